```python
import jax, jax.numpy as jnp
from jax import lax
import numpy as np

D_MODEL = 1024
BATCH = 8
SEQ = 2048
DEPTH = 1
DEC_BATCH = 16
DEC_SEQ = 64
PAST_LEN = 2048

CHUNK = 64
POOL_GROUPS = 4
POOL_WINDOWS = (2, 4, 8, 16)
POOL_MAX = 16
POOL_GDIM = D_MODEL // 8
POOL_WIDTH = POOL_GROUPS * POOL_GDIM
GMLP_HEADS = 4
GMLP_CHUNK = 128
GMLP_WIDTH = D_MODEL // 2
GMLP_HDIM = GMLP_WIDTH // GMLP_HEADS
IN_WIDTH = POOL_WIDTH + 2 * GMLP_WIDTH + 2 * D_MODEL
N_EXPERTS = 32
TOP_K = 4
D_FF = D_MODEL
SWIGLU_LIMIT = 7.0
SWIGLU_ALPHA = 1.702
EXPERT_BLOCK = 128
NORM_EPS = 1e-6
LN_EPS = 1e-5
N_MOD = 6

kernel_name = "pool_gmlp_moe_streaming_step"


def rms_norm(x, g):
    xf = x.astype(jnp.float32)
    y = xf * lax.rsqrt(jnp.mean(xf * xf, axis=-1, keepdims=True) + NORM_EPS)
    return (y * g.astype(jnp.float32)).astype(x.dtype)


def layer_norm(x, g, b):
    xf = x.astype(jnp.float32)
    mu = jnp.mean(xf, axis=-1, keepdims=True)
    var = jnp.mean(jnp.square(xf - mu), axis=-1, keepdims=True)
    y = (xf - mu) * lax.rsqrt(var + LN_EPS)
    return (y * g.astype(jnp.float32) + b.astype(jnp.float32)).astype(x.dtype)


def pool_mixer(a, hist, pos0, w_pool, s_pool):
    bsz, n_new, _ = a.shape
    n_hist = hist.shape[1]
    ext = jnp.concatenate([hist, a], axis=1).astype(jnp.float32)
    ext = ext.reshape(bsz, n_hist + n_new, POOL_GROUPS, POOL_GDIM)
    cs = jnp.cumsum(ext, axis=1)
    cs = jnp.concatenate([jnp.zeros((bsz, POOL_MAX, POOL_GROUPS, POOL_GDIM), jnp.float32), cs], axis=1)
    lo = POOL_MAX + n_hist
    pos = pos0 + jnp.arange(n_new)
    means = []
    for g, w in enumerate(POOL_WINDOWS):
        win_sum = cs[:, lo:lo + n_new, g] - cs[:, lo - w:lo - w + n_new, g]
        cnt = jnp.minimum(pos + 1, w).astype(jnp.float32)
        means.append(win_sum / cnt[None, :, None])
    mean = jnp.stack(means, axis=2)
    d = (mean - ext[:, n_hist:]).astype(a.dtype)
    y = jnp.einsum("blgc,gcd->blgd", d, w_pool)
    return y.reshape(bsz, n_new, POOL_WIDTH) * s_pool


def spatial_gate(u, v, w_s, b_s):
    bsz, n, _ = u.shape
    n_chunks = -(-n // GMLP_CHUNK)
    n_pad = n_chunks * GMLP_CHUNK
    vp = jnp.pad(v, ((0, 0), (0, n_pad - n), (0, 0)))
    vp = vp.reshape(bsz, n_chunks, GMLP_CHUNK, GMLP_HEADS, GMLP_HDIM)
    causal = jnp.tril(jnp.ones((GMLP_CHUNK, GMLP_CHUNK), dtype=bool))
    w_m = jnp.where(causal[None], w_s, jnp.zeros((), w_s.dtype))
    s = jnp.einsum("gqk,bnkgd->bnqgd", w_m, vp) + b_s.T[None, None, :, :, None]
    s = s.reshape(bsz, n_pad, GMLP_WIDTH)[:, :n]
    return u * s


def moe_ffn(h, w_router, b_router, w_up, b_up, w_down, b_down):
    n_tok, d = h.shape
    logits = (h @ w_router + b_router).astype(jnp.float32)
    top_vals, top_idx = lax.top_k(logits, TOP_K)
    gates = jax.nn.softmax(top_vals, axis=-1).astype(h.dtype)
    n_asg = n_tok * TOP_K
    e_flat = top_idx.reshape(n_asg)
    tok_flat = jnp.repeat(jnp.arange(n_tok, dtype=jnp.int32), TOP_K)
    g_flat = gates.reshape(n_asg)
    order = jnp.argsort(e_flat)
    e_sorted = e_flat[order]
    counts = jnp.bincount(e_flat, length=N_EXPERTS)
    padded = (counts + EXPERT_BLOCK - 1) // EXPERT_BLOCK * EXPERT_BLOCK
    starts = jnp.cumsum(counts) - counts
    pends = jnp.cumsum(padded)
    pstarts = pends - padded
    rank = jnp.arange(n_asg, dtype=jnp.int32) - starts[e_sorted]
    dest = pstarts[e_sorted] + rank
    n_blocks = -(-n_asg // EXPERT_BLOCK) + N_EXPERTS
    n_rows = n_blocks * EXPERT_BLOCK
    tok_buf = jnp.full((n_rows,), n_tok, jnp.int32).at[dest].set(tok_flat[order])
    g_buf = jnp.zeros((n_rows,), h.dtype).at[dest].set(g_flat[order])
    blk_start = jnp.arange(n_blocks, dtype=jnp.int32) * EXPERT_BLOCK
    blk_expert = jnp.minimum(jnp.searchsorted(pends, blk_start, side="right"), N_EXPERTS - 1)
    h_pad = jnp.concatenate([h, jnp.zeros((1, d), h.dtype)], axis=0)
    xb = h_pad[tok_buf].reshape(n_blocks, EXPERT_BLOCK, d)

    def expert_block(args):
        xe, e = args
        z = xe @ w_up[e] + b_up[e]
        zg, zl = z[:, :D_FF], z[:, D_FF:]
        zg = jnp.minimum(zg, SWIGLU_LIMIT)
        zl = jnp.clip(zl, -SWIGLU_LIMIT, SWIGLU_LIMIT)
        act = zg * jax.nn.sigmoid(SWIGLU_ALPHA * zg) * (zl + 1)
        return act @ w_down[e] + b_down[e]

    yb = lax.map(expert_block, (xb, blk_expert))
    y = jax.ops.segment_sum(yb.reshape(n_rows, d) * g_buf[:, None], tok_buf, num_segments=n_tok + 1)
    return y[:n_tok]


def trunk_layer(x, c, pool_hist, pos0, w_ada, b_ada, g_pre_mix, g_post_mix, g_pre_ffn, g_post_ffn,
                w_in, w_pool, s_pool, g_v, b_v, w_s, b_s, p_a, p_b, w_o,
                w_router, b_router, w_up, b_up, w_down, b_down):
    bsz, n, _ = x.shape
    mod = (jax.nn.silu(c) @ w_ada + b_ada).reshape(bsz, N_MOD, 1, D_MODEL)
    shift_m, scale_m, gate_m, shift_f, scale_f, gate_f = [mod[:, i] for i in range(N_MOD)]
    h = rms_norm(x, g_pre_mix) * (1 + scale_m) + shift_m
    proj = h @ w_in
    a, z, ga, gb = jnp.split(proj, [POOL_WIDTH, POOL_WIDTH + 2 * GMLP_WIDTH,
                                    POOL_WIDTH + 2 * GMLP_WIDTH + D_MODEL], axis=-1)
    y_a = pool_mixer(a, pool_hist, pos0, w_pool, s_pool)
    pool_state = jnp.concatenate([pool_hist, a], axis=1)[:, -(POOL_MAX - 1):]
    u, v = jnp.split(jax.nn.gelu(z, approximate=False), 2, axis=-1)
    v = layer_norm(v, g_v, b_v)
    y_b = spatial_gate(u, v, w_s, b_s)
    m = jax.nn.sigmoid(ga) * (y_a @ p_a) + jax.nn.sigmoid(gb) * (y_b @ p_b)
    x = x + gate_m * rms_norm(m @ w_o, g_post_mix)
    h = rms_norm(x, g_pre_ffn) * (1 + scale_f) + shift_f
    f = moe_ffn(h.reshape(bsz * n, D_MODEL), w_router, b_router, w_up, b_up, w_down, b_down)
    x = x + gate_f * rms_norm(f.reshape(bsz, n, D_MODEL), g_post_ffn)
    return x, pool_state, v


def setup_inputs(seed: int = 0) -> dict:
    key = jax.random.key(seed)
    k = jax.random.split(key, 32)

    def nrm(i, shape, scale):
        return jax.random.normal(k[i], shape, jnp.float32) * scale

    L = DEPTH
    return {
        "x_prompt": nrm(0, (BATCH, SEQ, D_MODEL), 1.0),
        "x_sample": nrm(1, (DEC_BATCH, DEC_SEQ, D_MODEL), 1.0),
        "cache_pool": nrm(2, (DEPTH, DEC_BATCH, POOL_MAX - 1, POOL_WIDTH), 1.0),
        "c_prompt": nrm(3, (BATCH, D_MODEL), 1.0),
        "c_sample": nrm(4, (DEC_BATCH, D_MODEL), 1.0),
        "w_ada": nrm(5, (L, D_MODEL, N_MOD * D_MODEL), 0.5 * D_MODEL ** -0.5),
        "b_ada": nrm(6, (L, N_MOD * D_MODEL), 0.02),
        "g_pre_mix": 1.0 + nrm(7, (L, D_MODEL), 0.05),
        "g_post_mix": 1.0 + nrm(8, (L, D_MODEL), 0.05),
        "g_pre_ffn": 1.0 + nrm(9, (L, D_MODEL), 0.05),
        "g_post_ffn": 1.0 + nrm(10, (L, D_MODEL), 0.05),
        "w_in": nrm(11, (L, D_MODEL, IN_WIDTH), D_MODEL ** -0.5),
        "w_pool": nrm(12, (L, POOL_GROUPS, POOL_GDIM, POOL_GDIM), POOL_GDIM ** -0.5),
        "s_pool": 1.0 + nrm(13, (L, POOL_WIDTH), 0.1),
        "g_v": 1.0 + nrm(14, (L, GMLP_WIDTH), 0.05),
        "b_v": nrm(15, (L, GMLP_WIDTH), 0.02),
        "w_s": nrm(16, (L, GMLP_HEADS, GMLP_CHUNK, GMLP_CHUNK), GMLP_CHUNK ** -0.5),
        "b_s": 1.0 + nrm(17, (L, GMLP_HEADS, GMLP_CHUNK), 0.05),
        "p_a": nrm(18, (L, POOL_WIDTH, D_MODEL), POOL_WIDTH ** -0.5),
        "p_b": nrm(19, (L, GMLP_WIDTH, D_MODEL), GMLP_WIDTH ** -0.5),
        "w_o": nrm(20, (L, D_MODEL, D_MODEL), D_MODEL ** -0.5),
        "w_router": nrm(21, (L, D_MODEL, N_EXPERTS), D_MODEL ** -0.5),
        "b_router": nrm(22, (L, N_EXPERTS), 0.01),
        "w_up": nrm(23, (L, N_EXPERTS, D_MODEL, 2 * D_FF), D_MODEL ** -0.5),
        "b_up": nrm(24, (L, N_EXPERTS, 2 * D_FF), 0.02),
        "w_down": nrm(25, (L, N_EXPERTS, D_FF, D_MODEL), D_FF ** -0.5),
        "b_down": nrm(26, (L, N_EXPERTS, D_MODEL), 0.02),
    }


def reference(x_prompt, x_sample, cache_pool, c_prompt, c_sample, w_ada, b_ada, g_pre_mix, g_post_mix,
              g_pre_ffn, g_post_ffn, w_in, w_pool, s_pool, g_v, b_v, w_s, b_s, p_a, p_b, w_o,
              w_router, b_router, w_up, b_up, w_down, b_down):
    xp, xs = x_prompt, x_sample
    pool_p, pool_s, v_s = [], [], []
    for l in range(DEPTH):
        lw = (w_ada[l], b_ada[l], g_pre_mix[l], g_post_mix[l], g_pre_ffn[l], g_post_ffn[l],
              w_in[l], w_pool[l], s_pool[l], g_v[l], b_v[l], w_s[l], b_s[l], p_a[l], p_b[l], w_o[l],
              w_router[l], b_router[l], w_up[l], b_up[l], w_down[l], b_down[l])
        no_hist = jnp.zeros((xp.shape[0], 0, POOL_WIDTH), xp.dtype)
        xp, sp, _ = trunk_layer(xp, c_prompt, no_hist, 0, *lw)
        xs, ss, vs = trunk_layer(xs, c_sample, cache_pool[l], PAST_LEN, *lw)
        pool_p.append(sp)
        pool_s.append(ss)
        v_s.append(vs)
    return (xp, xs, jnp.stack(pool_p), jnp.stack(pool_s), jnp.stack(v_s))
```

```python
import functools

import jax
import jax.numpy as jnp
from jax import lax
from jax.experimental import pallas as pl
from jax.experimental.pallas import tpu as pltpu

D_MODEL = 1024
POOL_WINDOWS = (2, 4, 8, 16)
POOL_MAX = 16
POOL_GDIM = 128
POOL_WIDTH = 512
GMLP_HEADS = 4
GMLP_CHUNK = 128
GMLP_WIDTH = 512
N_EXPERTS = 32
TOP_K = 4
D_FF = 1024
SWIGLU_LIMIT = 7.0
SWIGLU_ALPHA = 1.702
NORM_EPS = 1e-6
LN_EPS = 1e-5
N_MOD = 6

TM = 256
ROW_ALIGN = 8
RLOC = 1280
BM = 256
VMEM_LIMIT = 56 * 1024 * 1024

BF16 = jnp.bfloat16
F32 = jnp.float32


def _dot(a, b):
    return jnp.dot(a, b, preferred_element_type=F32)


def _rms(x, g):
    ms = jnp.mean(x * x, axis=-1, keepdims=True)
    return x * lax.rsqrt(ms + NORM_EPS) * g


def _gelu(x):
    return 0.5 * x * (1.0 + lax.erf(x * 0.7071067811865476))


def _ada_body(c_ref, w_ref, b_ref, o_ref):
    c = c_ref[...]
    s = c * jax.nn.sigmoid(c)
    o_ref[...] = _dot(s.astype(BF16), w_ref[...].astype(BF16)) + b_ref[...]


def _ada(c_all, w_ada, b_ada):
    n = c_all.shape[0]
    nt = N_MOD * D_MODEL // D_MODEL
    return pl.pallas_call(
        _ada_body,
        grid=(nt,),
        in_specs=[pl.BlockSpec((n, D_MODEL), lambda i: (0, 0)),
                  pl.BlockSpec((D_MODEL, D_MODEL), lambda i: (0, i)),
                  pl.BlockSpec((1, D_MODEL), lambda i: (0, i))],
        out_specs=pl.BlockSpec((n, D_MODEL), lambda i: (0, i)),
        out_shape=jax.ShapeDtypeStruct((n, N_MOD * D_MODEL), F32),
        compiler_params=pltpu.CompilerParams(dimension_semantics=("arbitrary",)),
        name="ada",
    )(c_all, w_ada, b_ada)


def _mixer_body(n_sub, sub_len, has_hist, pos0, *refs):
    it = iter(refs)
    x_ref = next(it)
    mod_ref = next(it)
    hist_ref = next(it) if has_hist else None
    (gpre_ref, win_ref, wpool_ref, spool_ref, gv_ref, bv_ref, ws_ref, bst_ref, pa_ref, pb_ref, wo_ref,
     gpost_ref, gffn_ref, wr_ref, br_ref) = [next(it) for _ in range(15)]
    x1_ref = next(it)
    h2_ref = next(it)
    pool_ref = next(it)
    v_ref = next(it) if has_hist else None
    info_ref = next(it)
    n8_ref = next(it)
    ext_ref = next(it)

    x = x_ref[...]
    mod = mod_ref[...]
    shift_m, scale_m, gate_m = mod[:, 0:1], mod[:, 1:2], mod[:, 2:3]
    shift_f, scale_f = mod[:, 3:4], mod[:, 4:5]

    h = _rms(x, gpre_ref[...]) * (1.0 + scale_m) + shift_m
    hb = h.reshape(TM, D_MODEL).astype(BF16)

    a = _dot(hb, win_ref[:, 0:POOL_WIDTH])
    if has_hist:
        for s in range(n_sub):
            ext_ref[s, 0:1, :] = jnp.zeros((1, POOL_WIDTH), F32)
            ext_ref[s, 1:POOL_MAX, :] = hist_ref[s]
            ext_ref[s, POOL_MAX:POOL_MAX + sub_len, :] = a[s * sub_len:(s + 1) * sub_len]
            pool_ref[s] = a[(s + 1) * sub_len - (POOL_MAX - 1):(s + 1) * sub_len]
        row0 = pos0
    else:
        j = pl.program_id(1)

        @pl.when(j == 0)
        def _():
            ext_ref[0, 0:POOL_MAX, :] = jnp.zeros((POOL_MAX, POOL_WIDTH), F32)

        @pl.when(j > 0)
        def _():
            ext_ref[0, 0:POOL_MAX, :] = ext_ref[0, sub_len:sub_len + POOL_MAX, :]

        ext_ref[0, POOL_MAX:POOL_MAX + sub_len, :] = a
        pool_ref[0] = a[sub_len - (POOL_MAX - 1):sub_len]
        row0 = pos0 + j * sub_len

    pos = row0 + lax.broadcasted_iota(jnp.int32, (sub_len, 1), 0)
    spool = spool_ref[...]
    ya_rows = []
    for s in range(n_sub):
        ya_groups = []
        for g, w in enumerate(POOL_WINDOWS):
            lanes = slice(g * POOL_GDIM, (g + 1) * POOL_GDIM)
            win = ext_ref[s, POOL_MAX:POOL_MAX + sub_len, lanes]
            for back in range(1, w):
                win = win + ext_ref[s, POOL_MAX - back:POOL_MAX - back + sub_len, lanes]
            cnt = jnp.minimum(pos + 1, w).astype(F32)
            dlt = win / cnt - a[s * sub_len:(s + 1) * sub_len, lanes]
            ya_groups.append(_dot(dlt.astype(BF16), wpool_ref[g]) * spool[:, lanes])
        ya_rows.append(jnp.concatenate(ya_groups, axis=1))
    y_a = jnp.concatenate(ya_rows, axis=0) if n_sub > 1 else ya_rows[0]

    u = _gelu(_dot(hb, win_ref[:, POOL_WIDTH:POOL_WIDTH + GMLP_WIDTH]))
    v = _gelu(_dot(hb, win_ref[:, POOL_WIDTH + GMLP_WIDTH:POOL_WIDTH + 2 * GMLP_WIDTH]))
    mu = jnp.mean(v, axis=-1, keepdims=True)
    vc = v - mu
    var = jnp.mean(vc * vc, axis=-1, keepdims=True)
    vn = vc * lax.rsqrt(var + LN_EPS) * gv_ref[...] + bv_ref[...]
    if has_hist:
        v_ref[...] = vn
    vb = vn.astype(BF16)
    seg = min(sub_len, GMLP_CHUNK)
    tri = (lax.broadcasted_iota(jnp.int32, (seg, seg), 0) >= lax.broadcasted_iota(jnp.int32, (seg, seg), 1))
    bst = bst_ref[...]
    yb_rows = []
    for c in range(TM // seg):
        rows = slice(c * seg, (c + 1) * seg)
        yb_heads = []
        for g in range(GMLP_HEADS):
            lanes = slice(g * 128, (g + 1) * 128)
            wm = jnp.where(tri, ws_ref[g, 0:seg, 0:seg], jnp.zeros((), BF16))
            sg = _dot(wm, vb[rows, lanes]) + bst[0:seg, g:g + 1]
            yb_heads.append(u[rows, lanes] * sg)
        yb_rows.append(jnp.concatenate(yb_heads, axis=1))
    y_b = jnp.concatenate(yb_rows, axis=0)

    c0 = POOL_WIDTH + 2 * GMLP_WIDTH
    ga = _dot(hb, win_ref[:, c0:c0 + D_MODEL])
    gb = _dot(hb, win_ref[:, c0 + D_MODEL:c0 + 2 * D_MODEL])
    m = (jax.nn.sigmoid(ga) * _dot(y_a.astype(BF16), pa_ref[...])
         + jax.nn.sigmoid(gb) * _dot(y_b.astype(BF16), pb_ref[...]))
    mo = _dot(m.astype(BF16), wo_ref[...]).reshape(n_sub, sub_len, D_MODEL)
    x1 = x + gate_m * _rms(mo, gpost_ref[...])
    x1_ref[...] = x1

    h2 = (_rms(x1, gffn_ref[...]) * (1.0 + scale_f) + shift_f).astype(BF16)
    h2_ref[...] = h2
    logits = _dot(h2.reshape(TM, D_MODEL), wr_ref[...]) + br_ref[...]

    lane = lax.broadcasted_iota(jnp.int32, (TM, N_EXPERTS), 1)
    work = logits
    onehots, vals = [], []
    for _ in range(TOP_K):
        mx = jnp.max(work, axis=-1, keepdims=True)
        idx = jnp.min(jnp.where(work == mx, lane, N_EXPERTS), axis=-1, keepdims=True)
        oh = lane == idx
        onehots.append(oh)
        vals.append(mx)
        work = jnp.where(oh, -jnp.inf, work)
    exps = [jnp.exp(vk - vals[0]) for vk in vals]
    den = exps[0] + exps[1] + exps[2] + exps[3]
    gates = [ek / den for ek in exps]

    sel = jnp.zeros((TM, N_EXPERTS), F32)
    for oh in onehots:
        sel = sel + oh.astype(F32)
    ltri = (lax.broadcasted_iota(jnp.int32, (TM, TM), 0) > lax.broadcasted_iota(jnp.int32, (TM, TM), 1))
    rank = _dot(ltri.astype(BF16), sel.astype(BF16))
    n_e = jnp.sum(sel, axis=0, keepdims=True).astype(jnp.int32)
    n8 = ((n_e + (ROW_ALIGN - 1)) // ROW_ALIGN) * ROW_ALIGN
    utri = (lax.broadcasted_iota(jnp.int32, (N_EXPERTS, N_EXPERTS), 0)
            < lax.broadcasted_iota(jnp.int32, (N_EXPERTS, N_EXPERTS), 1))
    n8b = jnp.broadcast_to(n8.astype(F32), (8, N_EXPERTS)).astype(BF16)
    off = _dot(n8b, utri.astype(BF16))[0:1, :]
    slot = off + rank
    lane128 = lax.broadcasted_iota(jnp.int32, (TM, 128), 1)
    info = jnp.zeros((TM, 128), F32)
    for k in range(TOP_K):
        p_k = jnp.sum(jnp.where(onehots[k], slot, 0.0), axis=-1, keepdims=True)
        info = info + jnp.where(lane128 == k, p_k, 0.0) + jnp.where(lane128 == TOP_K + k, gates[k], 0.0)
    info_ref[...] = info
    n8_ref[...] = n8.reshape(1, 1, N_EXPERTS)


def _mixer(x, mod, hist, weights, *, n_sub, sub_len, pos0):
    n_seq, seq_len, _ = x.shape
    has_hist = hist is not None
    tiles_per_seq = seq_len // sub_len if not has_hist else 1
    if has_hist:
        assert seq_len == sub_len and n_seq % n_sub == 0
        grid = (n_seq // n_sub, 1)
        n_tiles = n_seq // n_sub
        xmap = lambda i, j: (i, 0, 0)
        tile = lambda i, j: i
    else:
        assert n_sub == 1 and seq_len % sub_len == 0
        grid = (n_seq, tiles_per_seq)
        n_tiles = n_seq * tiles_per_seq
        xmap = lambda i, j: (i, j, 0)
        tile = lambda i, j: i * tiles_per_seq + j
    assert n_sub * sub_len == TM

    def full(arr):
        nd = arr.ndim
        return pl.BlockSpec(arr.shape, lambda i, j, _nd=nd: (0,) * _nd)

    in_specs = [pl.BlockSpec((n_sub, sub_len, D_MODEL), xmap),
                pl.BlockSpec((n_sub, N_MOD, D_MODEL), lambda i, j: (i, 0, 0))]
    args = [x, mod]
    if has_hist:
        in_specs.append(pl.BlockSpec((n_sub, POOL_MAX - 1, POOL_WIDTH), lambda i, j: (i, 0, 0)))
        args.append(hist)
    in_specs += [full(w) for w in weights]
    args += list(weights)

    out_shape = [jax.ShapeDtypeStruct(x.shape, F32),
                 jax.ShapeDtypeStruct(x.shape, BF16),
                 jax.ShapeDtypeStruct((n_seq, POOL_MAX - 1, POOL_WIDTH), F32)]
    out_specs = [pl.BlockSpec((n_sub, sub_len, D_MODEL), xmap),
                 pl.BlockSpec((n_sub, sub_len, D_MODEL), xmap),
                 pl.BlockSpec((n_sub, POOL_MAX - 1, POOL_WIDTH), lambda i, j: (i, 0, 0))]
    if has_hist:
        out_shape.append(jax.ShapeDtypeStruct((n_seq * seq_len, GMLP_WIDTH), F32))
        out_specs.append(pl.BlockSpec((TM, GMLP_WIDTH), lambda i, j: (i, 0)))
    out_shape += [jax.ShapeDtypeStruct((n_tiles * TM, 128), F32),
                  jax.ShapeDtypeStruct((n_tiles, 1, N_EXPERTS), jnp.int32)]
    out_specs += [pl.BlockSpec((TM, 128), lambda i, j: (tile(i, j), 0)),
                  pl.BlockSpec((1, 1, N_EXPERTS), lambda i, j: (tile(i, j), 0, 0))]

    return pl.pallas_call(
        functools.partial(_mixer_body, n_sub, sub_len, has_hist, pos0),
        grid=grid,
        in_specs=in_specs,
        out_specs=out_specs,
        out_shape=out_shape,
        scratch_shapes=[pltpu.VMEM((n_sub, POOL_MAX + sub_len, POOL_WIDTH), F32)],
        compiler_params=pltpu.CompilerParams(dimension_semantics=("arbitrary", "arbitrary"),
                                             vmem_limit_bytes=VMEM_LIMIT),
        name="mixer_hist" if has_hist else "mixer",
    )(*args)


def _slot_matrix(info, weights):
    cols = lax.broadcasted_iota(jnp.int32, (TM, RLOC), 1)
    out = jnp.zeros((TM, RLOC), F32)
    for k in range(TOP_K):
        p_k = info[:, k:k + 1].astype(jnp.int32)
        out = out + jnp.where(p_k == cols, weights[k], 0.0)
    return out.astype(BF16)


def _run_copy(src_ref, src_row, dst_ref, dst_row, n_rows, sem):
    return pltpu.make_async_copy(src_ref.at[pl.ds(src_row, n_rows), :],
                                 dst_ref.at[pl.ds(dst_row, n_rows), :], sem)


def _dispatch_body(n_tiles, do_fill, n8_s, off_s, base_s, tot_s, fill_lo_s, fill_n_s, *refs):
    if do_fill:
        h2_ref, info_ref, _xb_in, xb_ref, xs_ref, zero_ref, sems = refs
    else:
        h2_ref, info_ref, xb_ref, xs_ref, zero_ref, sems = refs
    i = pl.program_id(0)
    slot = i % 2

    def wait_tile(t, sl):
        tot = pl.multiple_of(tot_s[t], ROW_ALIGN)
        _run_copy(xs_ref.at[sl], 0, xb_ref, 0, tot, sems.at[sl]).wait()

    @pl.when(i >= 2)
    def _():
        wait_tile(i - 2, slot)

    info = info_ref[...]
    ones = [1.0] * TOP_K
    pt = _slot_matrix(info, ones)
    xs = lax.dot_general(pt, h2_ref[...].reshape(TM, D_MODEL), (((0,), (0,)), ((), ())),
                         preferred_element_type=F32)
    xs_ref[slot] = xs
    for e in range(N_EXPERTS):
        n = pl.multiple_of(n8_s[i * N_EXPERTS + e], ROW_ALIGN)

        @pl.when(n > 0)
        def _(e=e, n=n):
            src = pl.multiple_of(off_s[i * N_EXPERTS + e], ROW_ALIGN)
            dst = pl.multiple_of(base_s[i * N_EXPERTS + e], ROW_ALIGN)
            _run_copy(xs_ref.at[slot], src, xb_ref, dst, n, sems.at[slot]).start()

    @pl.when(i == n_tiles - 1)
    def _():
        if n_tiles >= 2:
            wait_tile(i - 1, 1 - slot)
        wait_tile(i, slot)
        if do_fill:
            zero_ref[...] = jnp.zeros(zero_ref.shape, F32)
            for e in range(N_EXPERTS):
                n = pl.multiple_of(fill_n_s[e], ROW_ALIGN)

                @pl.when(n > 0)
                def _(e=e, n=n):
                    dst = pl.multiple_of(fill_lo_s[e], ROW_ALIGN)
                    cp = _run_copy(zero_ref, 0, xb_ref, dst, n, sems.at[2])
                    cp.start()
                    cp.wait()


def _dispatch(h2, info, tables, xb, *, n_sub, sub_len, n_rows, do_fill):
    n_seq, seq_len, _ = h2.shape
    tiles_per_seq = seq_len // sub_len
    n_tiles = n_seq // n_sub * tiles_per_seq
    if n_sub == 1:
        hmap = lambda i, *_: (i // tiles_per_seq, i % tiles_per_seq, 0)
    else:
        hmap = lambda i, *_: (i, 0, 0)
    in_specs = [pl.BlockSpec((n_sub, sub_len, D_MODEL), hmap),
                pl.BlockSpec((TM, 128), lambda i, *_: (i, 0))]
    args = [h2, info]
    aliases = {}
    if xb is not None:
        in_specs.append(pl.BlockSpec(memory_space=pl.ANY))
        args.append(xb)
        aliases = {len(tables) + 2: 0}
    assert (xb is not None) == do_fill
    return pl.pallas_call(
        functools.partial(_dispatch_body, n_tiles, do_fill),
        grid_spec=pltpu.PrefetchScalarGridSpec(
            num_scalar_prefetch=len(tables),
            grid=(n_tiles,),
            in_specs=in_specs,
            out_specs=pl.BlockSpec(memory_space=pl.ANY),
            scratch_shapes=[pltpu.VMEM((2, RLOC, D_MODEL), F32),
                            pltpu.VMEM((BM, D_MODEL), F32),
                            pltpu.SemaphoreType.DMA((3,))],
        ),
        out_shape=jax.ShapeDtypeStruct((n_rows, D_MODEL), F32),
        input_output_aliases=aliases,
        compiler_params=pltpu.CompilerParams(dimension_semantics=("arbitrary",),
                                             vmem_limit_bytes=VMEM_LIMIT),
        name="dispatch_fill" if do_fill else "dispatch",
    )(*tables, *args)


def _expert_body(blk_e_s, blk_r_s, nblk_s, x_ref, wup_ref, bup_ref, wdn_ref, bdn_ref, y_ref, wup_bf, wdn_bf):
    b = pl.program_id(0)

    @pl.when(b < nblk_s[0])
    def _():
        changed = jnp.logical_or(b == 0, blk_e_s[b] != blk_e_s[jnp.maximum(b - 1, 0)])

        @pl.when(changed)
        def _():
            wup_bf[...] = wup_ref[...].astype(BF16)
            wdn_bf[...] = wdn_ref[...].astype(BF16)

        xv = x_ref[...].astype(BF16)
        z = _dot(xv, wup_bf[...]) + bup_ref[...]
        zg = jnp.minimum(z[:, :D_FF], SWIGLU_LIMIT)
        zl = jnp.clip(z[:, D_FF:], -SWIGLU_LIMIT, SWIGLU_LIMIT)
        act = zg * jax.nn.sigmoid(SWIGLU_ALPHA * zg) * (zl + 1.0)
        y_ref[...] = _dot(act.astype(BF16), wdn_bf[...]) + bdn_ref[...]


def _experts(xb, tables, w_up, b_up, w_down, b_down, n_blocks):
    n_rows = xb.shape[0]
    return pl.pallas_call(
        _expert_body,
        grid_spec=pltpu.PrefetchScalarGridSpec(
            num_scalar_prefetch=3,
            grid=(n_blocks,),
            in_specs=[pl.BlockSpec((BM, D_MODEL), lambda b, be, br, nb: (br[b], 0)),
                      pl.BlockSpec((None, D_MODEL, 2 * D_FF), lambda b, be, br, nb: (be[b], 0, 0)),
                      pl.BlockSpec((None, 1, 2 * D_FF), lambda b, be, br, nb: (be[b], 0, 0)),
                      pl.BlockSpec((None, D_FF, D_MODEL), lambda b, be, br, nb: (be[b], 0, 0)),
                      pl.BlockSpec((None, 1, D_MODEL), lambda b, be, br, nb: (be[b], 0, 0))],
            out_specs=pl.BlockSpec((BM, D_MODEL), lambda b, be, br, nb: (br[b], 0)),
            scratch_shapes=[pltpu.VMEM((D_MODEL, 2 * D_FF), BF16),
                            pltpu.VMEM((D_FF, D_MODEL), BF16)],
        ),
        out_shape=jax.ShapeDtypeStruct((n_rows, D_MODEL), F32),
        compiler_params=pltpu.CompilerParams(dimension_semantics=("arbitrary",),
                                             vmem_limit_bytes=VMEM_LIMIT),
        name="experts",
    )(*tables, xb, w_up, b_up, w_down, b_down)


def _combine_body(n_tiles, n8_s, off_s, base_s, tot_s, yb_ref, info_ref, x1_ref, mod_ref, gpost_ref, o_ref,
                  yloc_ref, sems):
    i = pl.program_id(0)
    slot = i % 2

    def start_tile(t, sl):
        for e in range(N_EXPERTS):
            n = pl.multiple_of(n8_s[t * N_EXPERTS + e], ROW_ALIGN)

            @pl.when(n > 0)
            def _(e=e, n=n):
                src = pl.multiple_of(base_s[t * N_EXPERTS + e], ROW_ALIGN)
                dst = pl.multiple_of(off_s[t * N_EXPERTS + e], ROW_ALIGN)
                _run_copy(yb_ref, src, yloc_ref.at[sl], dst, n, sems.at[sl]).start()

    @pl.when(i == 0)
    def _():
        yloc_ref[:, TM * TOP_K:RLOC, :] = jnp.zeros((2, RLOC - TM * TOP_K, D_MODEL), F32)
        start_tile(0, 0)

    @pl.when(i + 1 < n_tiles)
    def _():
        start_tile(i + 1, 1 - slot)

    tot = pl.multiple_of(tot_s[i], ROW_ALIGN)
    _run_copy(yb_ref, 0, yloc_ref.at[slot], 0, tot, sems.at[slot]).wait()

    info = info_ref[...]
    wmat = _slot_matrix(info, [info[:, TOP_K + k:TOP_K + k + 1] for k in range(TOP_K)])
    f = _dot(wmat, yloc_ref[slot].astype(BF16))
    x1 = x1_ref[...]
    gate_f = mod_ref[...][:, 5:6]
    o_ref[...] = x1 + gate_f * _rms(f.reshape(x1.shape), gpost_ref[...])


def _combine(yb, info, x1, mod, g_post, tables, *, n_sub, sub_len):
    n_seq, seq_len, _ = x1.shape
    tiles_per_seq = seq_len // sub_len
    n_tiles = n_seq // n_sub * tiles_per_seq
    if n_sub == 1:
        xmap = lambda i, *_: (i // tiles_per_seq, i % tiles_per_seq, 0)
        mmap = lambda i, *_: (i // tiles_per_seq, 0, 0)
    else:
        xmap = lambda i, *_: (i, 0, 0)
        mmap = lambda i, *_: (i, 0, 0)
    return pl.pallas_call(
        functools.partial(_combine_body, n_tiles),
        grid_spec=pltpu.PrefetchScalarGridSpec(
            num_scalar_prefetch=len(tables),
            grid=(n_tiles,),
            in_specs=[pl.BlockSpec(memory_space=pl.ANY),
                      pl.BlockSpec((TM, 128), lambda i, *_: (i, 0)),
                      pl.BlockSpec((n_sub, sub_len, D_MODEL), xmap),
                      pl.BlockSpec((n_sub, N_MOD, D_MODEL), mmap),
                      pl.BlockSpec((1, D_MODEL), lambda i, *_: (0, 0))],
            out_specs=pl.BlockSpec((n_sub, sub_len, D_MODEL), xmap),
            scratch_shapes=[pltpu.VMEM((2, RLOC, D_MODEL), F32),
                            pltpu.SemaphoreType.DMA((2,))],
        ),
        out_shape=jax.ShapeDtypeStruct(x1.shape, F32),
        compiler_params=pltpu.CompilerParams(dimension_semantics=("arbitrary",),
                                             vmem_limit_bytes=VMEM_LIMIT),
        name="combine",
    )(*tables, yb, info, x1, mod, g_post)


def _tables(n8_all):
    n_tiles = n8_all.shape[0]
    off = jnp.cumsum(n8_all, axis=1) - n8_all
    tot = jnp.sum(n8_all, axis=1)
    cnt = jnp.sum(n8_all, axis=0)
    region = (cnt + BM - 1) // BM * BM
    start = jnp.cumsum(region) - region
    base = start[None, :] + jnp.cumsum(n8_all, axis=0) - n8_all
    fill_lo = start + cnt
    fill_n = region - cnt
    ends = jnp.cumsum(region // BM)
    nblk = ends[-1]
    return dict(n8=n8_all, off=off, base=base, tot=tot, fill_lo=fill_lo, fill_n=fill_n, ends=ends, nblk=nblk)


def _max_rows(n_tokens, n_tiles):
    worst = n_tokens * TOP_K + n_tiles * N_EXPERTS * (ROW_ALIGN - 1) + N_EXPERTS * (BM - ROW_ALIGN)
    return (worst + BM - 1) // BM * BM


def kernel(x_prompt, x_sample, cache_pool, c_prompt, c_sample, w_ada, b_ada, g_pre_mix, g_post_mix, g_pre_ffn, g_post_ffn, w_in, w_pool, s_pool, g_v, b_v, w_s, b_s, p_a, p_b, w_o, w_router, b_router, w_up, b_up, w_down, b_down):
    assert w_ada.shape[0] == 1, "single-layer trunk"
    n_p, n_s = x_prompt.shape[0], x_sample.shape[0]
    past_len = x_prompt.shape[1]
    i32 = jnp.int32

    mod = _ada(jnp.concatenate([c_prompt, c_sample], axis=0), w_ada.reshape(D_MODEL, N_MOD * D_MODEL), b_ada)
    mod = mod.reshape(n_p + n_s, N_MOD, D_MODEL)
    mod_p, mod_s = mod[:n_p], mod[n_p:]

    row = lambda v: v.reshape(1, -1)
    weights = (row(g_pre_mix[0]), w_in[0].astype(BF16), w_pool[0].astype(BF16), row(s_pool[0]), row(g_v[0]),
               row(b_v[0]), w_s[0].astype(BF16), b_s[0].T, p_a[0].astype(BF16), p_b[0].astype(BF16),
               w_o[0].astype(BF16), row(g_post_mix[0]), row(g_pre_ffn[0]), w_router[0].astype(BF16),
               row(b_router[0]))

    sub_s = x_sample.shape[1]
    x1_p, h2_p, pool_p, info_p, n8_p = _mixer(x_prompt, mod_p, None, weights, n_sub=1, sub_len=TM, pos0=0)
    x1_s, h2_s, pool_s, v_s, info_s, n8_s = _mixer(x_sample, mod_s, cache_pool[0], weights,
                                                   n_sub=TM // sub_s, sub_len=sub_s, pos0=past_len)

    tiles_p, tiles_s = n8_p.shape[0], n8_s.shape[0]
    n_tiles = tiles_p + tiles_s
    tb = _tables(jnp.concatenate([n8_p, n8_s], axis=0).reshape(n_tiles, N_EXPERTS))
    n_rows = _max_rows(n_tiles * TM, n_tiles)
    n_blocks = n_rows // BM

    def tile_tables(lo, hi):
        return (tb["n8"][lo:hi].reshape(-1).astype(i32), tb["off"][lo:hi].reshape(-1).astype(i32),
                tb["base"][lo:hi].reshape(-1).astype(i32), tb["tot"][lo:hi].astype(i32))

    fill = (tb["fill_lo"].astype(i32), tb["fill_n"].astype(i32))
    xb = _dispatch(h2_p, info_p, tile_tables(0, tiles_p) + fill, None,
                   n_sub=1, sub_len=TM, n_rows=n_rows, do_fill=False)
    xb = _dispatch(h2_s, info_s, tile_tables(tiles_p, n_tiles) + fill, xb,
                   n_sub=TM // sub_s, sub_len=sub_s, n_rows=n_rows, do_fill=True)

    blk = jnp.minimum(jnp.arange(n_blocks, dtype=i32), tb["nblk"].astype(i32) - 1)
    blk_e = jnp.minimum(jnp.sum((tb["ends"][None, :] <= blk[:, None]).astype(i32), axis=1), N_EXPERTS - 1)
    yb = _experts(xb, (blk_e, blk, tb["nblk"].astype(i32).reshape(1)),
                  w_up.reshape(N_EXPERTS, D_MODEL, 2 * D_FF), b_up.reshape(N_EXPERTS, 1, 2 * D_FF),
                  w_down.reshape(N_EXPERTS, D_FF, D_MODEL), b_down.reshape(N_EXPERTS, 1, D_MODEL), n_blocks)

    g_post = row(g_post_ffn[0])
    y_p = _combine(yb, info_p, x1_p, mod_p, g_post, tile_tables(0, tiles_p), n_sub=1, sub_len=TM)
    y_s = _combine(yb, info_s, x1_s, mod_s, g_post, tile_tables(tiles_p, n_tiles),
                   n_sub=TM // sub_s, sub_len=sub_s)

    return (y_p, y_s, pool_p[None], pool_s[None], v_s.reshape(1, n_s, sub_s, GMLP_WIDTH))
```

```python
import functools

import jax
import jax.numpy as jnp
from jax import lax
from jax.experimental import pallas as pl
from jax.experimental.pallas import tpu as pltpu

D_MODEL = 1024
POOL_WINDOWS = (2, 4, 8, 16)
POOL_MAX = 16
POOL_GDIM = 128
POOL_WIDTH = 512
GMLP_HEADS = 4
GMLP_CHUNK = 128
GMLP_WIDTH = 512
N_EXPERTS = 32
TOP_K = 4
D_FF = 1024
SWIGLU_LIMIT = 7.0
SWIGLU_ALPHA = 1.702
NORM_EPS = 1e-6
LN_EPS = 1e-5
N_MOD = 6

TM = 256
ROW_ALIGN = 8
RLOC = 1280
BM = 256
EXPERT_CHAINS = 1
VMEM_LIMIT = 56 * 1024 * 1024

BF16 = jnp.bfloat16
F32 = jnp.float32


def _dot(a, b):
    return jnp.dot(a, b, preferred_element_type=F32)


def _rms(x, g):
    ms = jnp.mean(x * x, axis=-1, keepdims=True)
    return x * lax.rsqrt(ms + NORM_EPS) * g


def _sigmoid(x):
    return 0.5 * jnp.tanh(0.5 * x) + 0.5


def _gelu(x):
    return 0.5 * x * (1.0 + lax.erf(x * 0.7071067811865476))


def _ada_body(c_ref, w_ref, b_ref, o_ref):
    c = c_ref[...]
    s = c * jax.nn.sigmoid(c)
    o_ref[...] = _dot(s.astype(BF16), w_ref[...].astype(BF16)) + b_ref[...]


def _ada(c_all, w_ada, b_ada):
    n = c_all.shape[0]
    nt = N_MOD * D_MODEL // D_MODEL
    return pl.pallas_call(
        _ada_body,
        grid=(nt,),
        in_specs=[pl.BlockSpec((n, D_MODEL), lambda i: (0, 0)),
                  pl.BlockSpec((D_MODEL, D_MODEL), lambda i: (0, i)),
                  pl.BlockSpec((1, D_MODEL), lambda i: (0, i))],
        out_specs=pl.BlockSpec((n, D_MODEL), lambda i: (0, i)),
        out_shape=jax.ShapeDtypeStruct((n, N_MOD * D_MODEL), F32),
        compiler_params=pltpu.CompilerParams(dimension_semantics=("arbitrary",)),
        name="ada",
    )(c_all, w_ada, b_ada)


def _mixer_body(n_sub, sub_len, has_hist, pos0, *refs):
    it = iter(refs)
    x_ref = next(it)
    mod_ref = next(it)
    hist_ref = next(it) if has_hist else None
    (gpre_ref, win_ref, wpool_ref, spool_ref, gv_ref, bv_ref, ws_ref, bst_ref, pa_ref, pb_ref, wo_ref,
     gpost_ref, gffn_ref, wr_ref, br_ref) = [next(it) for _ in range(15)]
    x1_ref = next(it)
    h2_ref = next(it)
    pool_ref = next(it)
    v_ref = next(it) if has_hist else None
    info_ref = next(it)
    n8_ref = next(it)
    ext_ref = next(it)

    x = x_ref[...]
    mod = mod_ref[...]
    shift_m, scale_m, gate_m = mod[:, 0:1], mod[:, 1:2], mod[:, 2:3]
    shift_f, scale_f = mod[:, 3:4], mod[:, 4:5]

    h = _rms(x, gpre_ref[...]) * (1.0 + scale_m) + shift_m
    hb = h.reshape(TM, D_MODEL).astype(BF16)

    a = _dot(hb, win_ref[:, 0:POOL_WIDTH])
    if has_hist:
        for s in range(n_sub):
            ext_ref[s, 0:1, :] = jnp.zeros((1, POOL_WIDTH), F32)
            ext_ref[s, 1:POOL_MAX, :] = hist_ref[s]
            ext_ref[s, POOL_MAX:POOL_MAX + sub_len, :] = a[s * sub_len:(s + 1) * sub_len]
            pool_ref[s] = a[(s + 1) * sub_len - (POOL_MAX - 1):(s + 1) * sub_len]
        row0 = pos0
    else:
        j = pl.program_id(1)

        @pl.when(j == 0)
        def _():
            ext_ref[0, 0:POOL_MAX, :] = jnp.zeros((POOL_MAX, POOL_WIDTH), F32)

        @pl.when(j > 0)
        def _():
            ext_ref[0, 0:POOL_MAX, :] = ext_ref[0, sub_len:sub_len + POOL_MAX, :]

        ext_ref[0, POOL_MAX:POOL_MAX + sub_len, :] = a
        pool_ref[0] = a[sub_len - (POOL_MAX - 1):sub_len]
        row0 = pos0 + j * sub_len

    pos = row0 + lax.broadcasted_iota(jnp.int32, (sub_len, 1), 0)
    spool = spool_ref[...]
    ya_rows = []
    for s in range(n_sub):
        ya_groups = []
        for g, w in enumerate(POOL_WINDOWS):
            lanes = slice(g * POOL_GDIM, (g + 1) * POOL_GDIM)
            win = ext_ref[s, POOL_MAX:POOL_MAX + sub_len, lanes]
            for back in range(1, w):
                win = win + ext_ref[s, POOL_MAX - back:POOL_MAX - back + sub_len, lanes]
            cnt = jnp.minimum(pos + 1, w).astype(F32)
            dlt = win / cnt - a[s * sub_len:(s + 1) * sub_len, lanes]
            ya_groups.append(_dot(dlt.astype(BF16), wpool_ref[g]) * spool[:, lanes])
        ya_rows.append(jnp.concatenate(ya_groups, axis=1))
    y_a = jnp.concatenate(ya_rows, axis=0) if n_sub > 1 else ya_rows[0]

    u = _gelu(_dot(hb, win_ref[:, POOL_WIDTH:POOL_WIDTH + GMLP_WIDTH]))
    v = _gelu(_dot(hb, win_ref[:, POOL_WIDTH + GMLP_WIDTH:POOL_WIDTH + 2 * GMLP_WIDTH]))
    mu = jnp.mean(v, axis=-1, keepdims=True)
    vc = v - mu
    var = jnp.mean(vc * vc, axis=-1, keepdims=True)
    vn = vc * lax.rsqrt(var + LN_EPS) * gv_ref[...] + bv_ref[...]
    if has_hist:
        v_ref[...] = vn
    vb = vn.astype(BF16)
    seg = min(sub_len, GMLP_CHUNK)
    tri = (lax.broadcasted_iota(jnp.int32, (seg, seg), 0) >= lax.broadcasted_iota(jnp.int32, (seg, seg), 1))
    bst = bst_ref[...]
    yb_rows = []
    for c in range(TM // seg):
        rows = slice(c * seg, (c + 1) * seg)
        yb_heads = []
        for g in range(GMLP_HEADS):
            lanes = slice(g * 128, (g + 1) * 128)
            wm = jnp.where(tri, ws_ref[g, 0:seg, 0:seg], jnp.zeros((), BF16))
            sg = _dot(wm, vb[rows, lanes]) + bst[0:seg, g:g + 1]
            yb_heads.append(u[rows, lanes] * sg)
        yb_rows.append(jnp.concatenate(yb_heads, axis=1))
    y_b = jnp.concatenate(yb_rows, axis=0)

    c0 = POOL_WIDTH + 2 * GMLP_WIDTH
    ga = _dot(hb, win_ref[:, c0:c0 + D_MODEL])
    gb = _dot(hb, win_ref[:, c0 + D_MODEL:c0 + 2 * D_MODEL])
    m = (_sigmoid(ga) * _dot(y_a.astype(BF16), pa_ref[...])
         + _sigmoid(gb) * _dot(y_b.astype(BF16), pb_ref[...]))
    mo = _dot(m.astype(BF16), wo_ref[...]).reshape(n_sub, sub_len, D_MODEL)
    x1 = x + gate_m * _rms(mo, gpost_ref[...])
    x1_ref[...] = x1

    h2 = (_rms(x1, gffn_ref[...]) * (1.0 + scale_f) + shift_f).astype(BF16)
    h2_ref[...] = h2
    logits = _dot(h2.reshape(TM, D_MODEL), wr_ref[...]) + br_ref[...]

    lane = lax.broadcasted_iota(jnp.int32, (TM, N_EXPERTS), 1)
    work = logits
    onehots, vals = [], []
    for _ in range(TOP_K):
        mx = jnp.max(work, axis=-1, keepdims=True)
        idx = jnp.min(jnp.where(work == mx, lane, N_EXPERTS), axis=-1, keepdims=True)
        oh = lane == idx
        onehots.append(oh)
        vals.append(mx)
        work = jnp.where(oh, -jnp.inf, work)
    exps = [jnp.exp(vk - vals[0]) for vk in vals]
    den = exps[0] + exps[1] + exps[2] + exps[3]
    gates = [ek / den for ek in exps]

    sel = jnp.zeros((TM, N_EXPERTS), F32)
    for oh in onehots:
        sel = sel + oh.astype(F32)
    ltri = (lax.broadcasted_iota(jnp.int32, (TM, TM), 0) > lax.broadcasted_iota(jnp.int32, (TM, TM), 1))
    rank = _dot(ltri.astype(BF16), sel.astype(BF16))
    n_e = jnp.sum(sel, axis=0, keepdims=True).astype(jnp.int32)
    n8 = ((n_e + (ROW_ALIGN - 1)) // ROW_ALIGN) * ROW_ALIGN
    utri = (lax.broadcasted_iota(jnp.int32, (N_EXPERTS, N_EXPERTS), 0)
            < lax.broadcasted_iota(jnp.int32, (N_EXPERTS, N_EXPERTS), 1))
    n8b = jnp.broadcast_to(n8.astype(F32), (8, N_EXPERTS)).astype(BF16)
    off = _dot(n8b, utri.astype(BF16))[0:1, :]
    slot = off + rank
    lane128 = lax.broadcasted_iota(jnp.int32, (TM, 128), 1)
    info = jnp.zeros((TM, 128), F32)
    for k in range(TOP_K):
        p_k = jnp.sum(jnp.where(onehots[k], slot, 0.0), axis=-1, keepdims=True)
        info = info + jnp.where(lane128 == k, p_k, 0.0) + jnp.where(lane128 == TOP_K + k, gates[k], 0.0)
    info_ref[...] = info
    n8_ref[...] = n8.reshape(1, 1, N_EXPERTS)


def _mixer(x, mod, hist, weights, *, n_sub, sub_len, pos0):
    n_seq, seq_len, _ = x.shape
    has_hist = hist is not None
    tiles_per_seq = seq_len // sub_len if not has_hist else 1
    if has_hist:
        assert seq_len == sub_len and n_seq % n_sub == 0
        grid = (n_seq // n_sub, 1)
        n_tiles = n_seq // n_sub
        xmap = lambda i, j: (i, 0, 0)
        tile = lambda i, j: i
    else:
        assert n_sub == 1 and seq_len % sub_len == 0
        grid = (n_seq, tiles_per_seq)
        n_tiles = n_seq * tiles_per_seq
        xmap = lambda i, j: (i, j, 0)
        tile = lambda i, j: i * tiles_per_seq + j
    assert n_sub * sub_len == TM

    def full(arr):
        nd = arr.ndim
        return pl.BlockSpec(arr.shape, lambda i, j, _nd=nd: (0,) * _nd)

    in_specs = [pl.BlockSpec((n_sub, sub_len, D_MODEL), xmap),
                pl.BlockSpec((n_sub, N_MOD, D_MODEL), lambda i, j: (i, 0, 0))]
    args = [x, mod]
    if has_hist:
        in_specs.append(pl.BlockSpec((n_sub, POOL_MAX - 1, POOL_WIDTH), lambda i, j: (i, 0, 0)))
        args.append(hist)
    in_specs += [full(w) for w in weights]
    args += list(weights)

    out_shape = [jax.ShapeDtypeStruct(x.shape, F32),
                 jax.ShapeDtypeStruct(x.shape, BF16),
                 jax.ShapeDtypeStruct((n_seq, POOL_MAX - 1, POOL_WIDTH), F32)]
    out_specs = [pl.BlockSpec((n_sub, sub_len, D_MODEL), xmap),
                 pl.BlockSpec((n_sub, sub_len, D_MODEL), xmap),
                 pl.BlockSpec((n_sub, POOL_MAX - 1, POOL_WIDTH), lambda i, j: (i, 0, 0))]
    if has_hist:
        out_shape.append(jax.ShapeDtypeStruct((n_seq * seq_len, GMLP_WIDTH), F32))
        out_specs.append(pl.BlockSpec((TM, GMLP_WIDTH), lambda i, j: (i, 0)))
    out_shape += [jax.ShapeDtypeStruct((n_tiles * TM, 128), F32),
                  jax.ShapeDtypeStruct((n_tiles, 1, N_EXPERTS), jnp.int32)]
    out_specs += [pl.BlockSpec((TM, 128), lambda i, j: (tile(i, j), 0)),
                  pl.BlockSpec((1, 1, N_EXPERTS), lambda i, j: (tile(i, j), 0, 0))]

    return pl.pallas_call(
        functools.partial(_mixer_body, n_sub, sub_len, has_hist, pos0),
        grid=grid,
        in_specs=in_specs,
        out_specs=out_specs,
        out_shape=out_shape,
        scratch_shapes=[pltpu.VMEM((n_sub, POOL_MAX + sub_len, POOL_WIDTH), F32)],
        compiler_params=pltpu.CompilerParams(dimension_semantics=("arbitrary", "arbitrary"),
                                             vmem_limit_bytes=VMEM_LIMIT),
        name="mixer_hist" if has_hist else "mixer",
    )(*args)


def _slot_matrix(info, weights):
    cols = lax.broadcasted_iota(jnp.int32, (TM, RLOC), 1)
    out = jnp.zeros((TM, RLOC), F32)
    for k in range(TOP_K):
        p_k = info[:, k:k + 1].astype(jnp.int32)
        out = out + jnp.where(p_k == cols, weights[k], 0.0)
    return out.astype(BF16)


def _run_copy(src_ref, src_row, dst_ref, dst_row, n_rows, sem):
    return pltpu.make_async_copy(src_ref.at[pl.ds(src_row, n_rows), :],
                                 dst_ref.at[pl.ds(dst_row, n_rows), :], sem)


def _dispatch_body(n_tiles, do_fill, n8_s, off_s, base_s, tot_s, fill_lo_s, fill_n_s, *refs):
    if do_fill:
        h2_ref, info_ref, _xb_in, xb_ref, xs_ref, zero_ref, sems = refs
    else:
        h2_ref, info_ref, xb_ref, xs_ref, zero_ref, sems = refs
    i = pl.program_id(0)
    slot = i % 2

    def wait_tile(t, sl):
        tot = pl.multiple_of(tot_s[t], ROW_ALIGN)
        _run_copy(xs_ref.at[sl], 0, xb_ref, 0, tot, sems.at[sl]).wait()

    @pl.when(i >= 2)
    def _():
        wait_tile(i - 2, slot)

    info = info_ref[...]
    ones = [1.0] * TOP_K
    pt = _slot_matrix(info, ones)
    xs = lax.dot_general(pt, h2_ref[...].reshape(TM, D_MODEL), (((0,), (0,)), ((), ())),
                         preferred_element_type=F32)
    xs_ref[slot] = xs
    for e in range(N_EXPERTS):
        n = pl.multiple_of(n8_s[i * N_EXPERTS + e], ROW_ALIGN)

        @pl.when(n > 0)
        def _(e=e, n=n):
            src = pl.multiple_of(off_s[i * N_EXPERTS + e], ROW_ALIGN)
            dst = pl.multiple_of(base_s[i * N_EXPERTS + e], ROW_ALIGN)
            _run_copy(xs_ref.at[slot], src, xb_ref, dst, n, sems.at[slot]).start()

    @pl.when(i == n_tiles - 1)
    def _():
        if n_tiles >= 2:
            wait_tile(i - 1, 1 - slot)
        wait_tile(i, slot)
        if do_fill:
            zero_ref[...] = jnp.zeros(zero_ref.shape, F32)
            for e in range(N_EXPERTS):
                n = pl.multiple_of(fill_n_s[e], ROW_ALIGN)

                @pl.when(n > 0)
                def _(e=e, n=n):
                    dst = pl.multiple_of(fill_lo_s[e], ROW_ALIGN)
                    cp = _run_copy(zero_ref, 0, xb_ref, dst, n, sems.at[2])
                    cp.start()
                    cp.wait()


def _dispatch(h2, info, tables, xb, *, n_sub, sub_len, n_rows, do_fill):
    n_seq, seq_len, _ = h2.shape
    tiles_per_seq = seq_len // sub_len
    n_tiles = n_seq // n_sub * tiles_per_seq
    if n_sub == 1:
        hmap = lambda i, *_: (i // tiles_per_seq, i % tiles_per_seq, 0)
    else:
        hmap = lambda i, *_: (i, 0, 0)
    in_specs = [pl.BlockSpec((n_sub, sub_len, D_MODEL), hmap),
                pl.BlockSpec((TM, 128), lambda i, *_: (i, 0))]
    args = [h2, info]
    aliases = {}
    if xb is not None:
        in_specs.append(pl.BlockSpec(memory_space=pl.ANY))
        args.append(xb)
        aliases = {len(tables) + 2: 0}
    assert (xb is not None) == do_fill
    return pl.pallas_call(
        functools.partial(_dispatch_body, n_tiles, do_fill),
        grid_spec=pltpu.PrefetchScalarGridSpec(
            num_scalar_prefetch=len(tables),
            grid=(n_tiles,),
            in_specs=in_specs,
            out_specs=pl.BlockSpec(memory_space=pl.ANY),
            scratch_shapes=[pltpu.VMEM((2, RLOC, D_MODEL), F32),
                            pltpu.VMEM((BM, D_MODEL), F32),
                            pltpu.SemaphoreType.DMA((3,))],
        ),
        out_shape=jax.ShapeDtypeStruct((n_rows, D_MODEL), F32),
        input_output_aliases=aliases,
        compiler_params=pltpu.CompilerParams(dimension_semantics=("arbitrary",),
                                             vmem_limit_bytes=VMEM_LIMIT),
        name="dispatch_fill" if do_fill else "dispatch",
    )(*tables, *args)


def _expert_rows(xv, wup_bf, bup, wdn_bf, bdn):
    z = _dot(xv.astype(BF16), wup_bf[...]) + bup
    zg = jnp.minimum(z[:, :D_FF], SWIGLU_LIMIT)
    zl = jnp.clip(z[:, D_FF:], -SWIGLU_LIMIT, SWIGLU_LIMIT)
    act = zg * _sigmoid(SWIGLU_ALPHA * zg) * (zl + 1.0)
    return _dot(act.astype(BF16), wdn_bf[...]) + bdn


def _expert_body(ends_s, xb_ref, wup_ref, bup_ref, wdn_ref, bdn_ref, yb_ref, x_buf, y_buf, wup_bf, wdn_bf,
                 x_sem, y_sem):
    e = pl.program_id(0)
    n_total = ends_s[N_EXPERTS - 1]
    lo = jnp.where(e == 0, 0, ends_s[jnp.maximum(e - 1, 0)])
    hi = ends_s[e]

    def x_copy(blk, slot):
        return _run_copy(xb_ref, pl.multiple_of(blk * BM, BM), x_buf.at[slot], 0, BM, x_sem.at[slot])

    def y_copy(blk, slot):
        return _run_copy(y_buf.at[slot], 0, yb_ref, pl.multiple_of(blk * BM, BM), BM, y_sem.at[slot])

    @pl.when(e == 0)
    def _():
        x_copy(0, 0).start()

    @pl.when(hi > lo)
    def _():
        wup_bf[...] = wup_ref[...].astype(BF16)
        wdn_bf[...] = wdn_ref[...].astype(BF16)
        bup = bup_ref[...]
        bdn = bdn_ref[...]

        def block(blk, carry):
            slot = blk % 2
            x_copy(blk, slot).wait()

            @pl.when(blk + 1 < n_total)
            def _():
                x_copy(blk + 1, 1 - slot).start()

            @pl.when(blk >= 2)
            def _():
                y_copy(blk - 2, slot).wait()

            sub = BM // EXPERT_CHAINS
            for c in range(EXPERT_CHAINS):
                rows = pl.ds(c * sub, sub)
                y_buf[slot, rows, :] = _expert_rows(x_buf[slot, rows, :], wup_bf, bup, wdn_bf, bdn)
            y_copy(blk, slot).start()
            return carry

        lax.fori_loop(lo, hi, block, 0)

    @pl.when(e == N_EXPERTS - 1)
    def _():
        y_copy(n_total - 2, n_total % 2).wait()
        y_copy(n_total - 1, (n_total - 1) % 2).wait()


def _experts(xb, ends, w_up, b_up, w_down, b_down):
    n_rows = xb.shape[0]
    wmap = lambda e, ends: (e, 0, 0)
    return pl.pallas_call(
        _expert_body,
        grid_spec=pltpu.PrefetchScalarGridSpec(
            num_scalar_prefetch=1,
            grid=(N_EXPERTS,),
            in_specs=[pl.BlockSpec(memory_space=pl.ANY),
                      pl.BlockSpec((None, D_MODEL, 2 * D_FF), wmap),
                      pl.BlockSpec((None, 1, 2 * D_FF), wmap),
                      pl.BlockSpec((None, D_FF, D_MODEL), wmap),
                      pl.BlockSpec((None, 1, D_MODEL), wmap)],
            out_specs=pl.BlockSpec(memory_space=pl.ANY),
            scratch_shapes=[pltpu.VMEM((2, BM, D_MODEL), F32),
                            pltpu.VMEM((2, BM, D_MODEL), F32),
                            pltpu.VMEM((D_MODEL, 2 * D_FF), BF16),
                            pltpu.VMEM((D_FF, D_MODEL), BF16),
                            pltpu.SemaphoreType.DMA((2,)),
                            pltpu.SemaphoreType.DMA((2,))],
        ),
        out_shape=jax.ShapeDtypeStruct((n_rows, D_MODEL), F32),
        compiler_params=pltpu.CompilerParams(dimension_semantics=("arbitrary",),
                                             vmem_limit_bytes=VMEM_LIMIT),
        name="experts",
    )(ends, xb, w_up, b_up, w_down, b_down)


def _combine_body(n_tiles, n8_s, off_s, base_s, tot_s, yb_ref, info_ref, x1_ref, mod_ref, gpost_ref, o_ref,
                  yloc_ref, sems):
    i = pl.program_id(0)
    slot = i % 2

    def start_tile(t, sl):
        for e in range(N_EXPERTS):
            n = pl.multiple_of(n8_s[t * N_EXPERTS + e], ROW_ALIGN)

            @pl.when(n > 0)
            def _(e=e, n=n):
                src = pl.multiple_of(base_s[t * N_EXPERTS + e], ROW_ALIGN)
                dst = pl.multiple_of(off_s[t * N_EXPERTS + e], ROW_ALIGN)
                _run_copy(yb_ref, src, yloc_ref.at[sl], dst, n, sems.at[sl]).start()

    @pl.when(i == 0)
    def _():
        yloc_ref[:, TM * TOP_K:RLOC, :] = jnp.zeros((2, RLOC - TM * TOP_K, D_MODEL), F32)
        start_tile(0, 0)

    @pl.when(i + 1 < n_tiles)
    def _():
        start_tile(i + 1, 1 - slot)

    tot = pl.multiple_of(tot_s[i], ROW_ALIGN)
    _run_copy(yb_ref, 0, yloc_ref.at[slot], 0, tot, sems.at[slot]).wait()

    info = info_ref[...]
    wmat = _slot_matrix(info, [info[:, TOP_K + k:TOP_K + k + 1] for k in range(TOP_K)])
    f = _dot(wmat, yloc_ref[slot].astype(BF16))
    x1 = x1_ref[...]
    gate_f = mod_ref[...][:, 5:6]
    o_ref[...] = x1 + gate_f * _rms(f.reshape(x1.shape), gpost_ref[...])


def _combine(yb, info, x1, mod, g_post, tables, *, n_sub, sub_len):
    n_seq, seq_len, _ = x1.shape
    tiles_per_seq = seq_len // sub_len
    n_tiles = n_seq // n_sub * tiles_per_seq
    if n_sub == 1:
        xmap = lambda i, *_: (i // tiles_per_seq, i % tiles_per_seq, 0)
        mmap = lambda i, *_: (i // tiles_per_seq, 0, 0)
    else:
        xmap = lambda i, *_: (i, 0, 0)
        mmap = lambda i, *_: (i, 0, 0)
    return pl.pallas_call(
        functools.partial(_combine_body, n_tiles),
        grid_spec=pltpu.PrefetchScalarGridSpec(
            num_scalar_prefetch=len(tables),
            grid=(n_tiles,),
            in_specs=[pl.BlockSpec(memory_space=pl.ANY),
                      pl.BlockSpec((TM, 128), lambda i, *_: (i, 0)),
                      pl.BlockSpec((n_sub, sub_len, D_MODEL), xmap),
                      pl.BlockSpec((n_sub, N_MOD, D_MODEL), mmap),
                      pl.BlockSpec((1, D_MODEL), lambda i, *_: (0, 0))],
            out_specs=pl.BlockSpec((n_sub, sub_len, D_MODEL), xmap),
            scratch_shapes=[pltpu.VMEM((2, RLOC, D_MODEL), F32),
                            pltpu.SemaphoreType.DMA((2,))],
        ),
        out_shape=jax.ShapeDtypeStruct(x1.shape, F32),
        compiler_params=pltpu.CompilerParams(dimension_semantics=("arbitrary",),
                                             vmem_limit_bytes=VMEM_LIMIT),
        name="combine",
    )(*tables, yb, info, x1, mod, g_post)


def _tables(n8_all):
    n_tiles = n8_all.shape[0]
    off = jnp.cumsum(n8_all, axis=1) - n8_all
    tot = jnp.sum(n8_all, axis=1)
    cnt = jnp.sum(n8_all, axis=0)
    region = (cnt + BM - 1) // BM * BM
    start = jnp.cumsum(region) - region
    base = start[None, :] + jnp.cumsum(n8_all, axis=0) - n8_all
    fill_lo = start + cnt
    fill_n = region - cnt
    ends = jnp.cumsum(region // BM)
    nblk = ends[-1]
    return dict(n8=n8_all, off=off, base=base, tot=tot, fill_lo=fill_lo, fill_n=fill_n, ends=ends, nblk=nblk)


def _max_rows(n_tokens, n_tiles):
    worst = n_tokens * TOP_K + n_tiles * N_EXPERTS * (ROW_ALIGN - 1) + N_EXPERTS * (BM - ROW_ALIGN)
    return (worst + BM - 1) // BM * BM


def kernel(x_prompt, x_sample, cache_pool, c_prompt, c_sample, w_ada, b_ada, g_pre_mix, g_post_mix, g_pre_ffn, g_post_ffn, w_in, w_pool, s_pool, g_v, b_v, w_s, b_s, p_a, p_b, w_o, w_router, b_router, w_up, b_up, w_down, b_down):
    assert w_ada.shape[0] == 1, "single-layer trunk"
    n_p, n_s = x_prompt.shape[0], x_sample.shape[0]
    past_len = x_prompt.shape[1]
    i32 = jnp.int32

    mod = _ada(jnp.concatenate([c_prompt, c_sample], axis=0), w_ada.reshape(D_MODEL, N_MOD * D_MODEL), b_ada)
    mod = mod.reshape(n_p + n_s, N_MOD, D_MODEL)
    mod_p, mod_s = mod[:n_p], mod[n_p:]

    row = lambda v: v.reshape(1, -1)
    weights = (row(g_pre_mix[0]), w_in[0].astype(BF16), w_pool[0].astype(BF16), row(s_pool[0]), row(g_v[0]),
               row(b_v[0]), w_s[0].astype(BF16), b_s[0].T, p_a[0].astype(BF16), p_b[0].astype(BF16),
               w_o[0].astype(BF16), row(g_post_mix[0]), row(g_pre_ffn[0]), w_router[0].astype(BF16),
               row(b_router[0]))

    sub_s = x_sample.shape[1]
    x1_p, h2_p, pool_p, info_p, n8_p = _mixer(x_prompt, mod_p, None, weights, n_sub=1, sub_len=TM, pos0=0)
    x1_s, h2_s, pool_s, v_s, info_s, n8_s = _mixer(x_sample, mod_s, cache_pool[0], weights,
                                                   n_sub=TM // sub_s, sub_len=sub_s, pos0=past_len)

    tiles_p, tiles_s = n8_p.shape[0], n8_s.shape[0]
    n_tiles = tiles_p + tiles_s
    tb = _tables(jnp.concatenate([n8_p, n8_s], axis=0).reshape(n_tiles, N_EXPERTS))
    n_rows = _max_rows(n_tiles * TM, n_tiles)
    n_blocks = n_rows // BM

    def tile_tables(lo, hi):
        return (tb["n8"][lo:hi].reshape(-1).astype(i32), tb["off"][lo:hi].reshape(-1).astype(i32),
                tb["base"][lo:hi].reshape(-1).astype(i32), tb["tot"][lo:hi].astype(i32))

    fill = (tb["fill_lo"].astype(i32), tb["fill_n"].astype(i32))
    xb = _dispatch(h2_p, info_p, tile_tables(0, tiles_p) + fill, None,
                   n_sub=1, sub_len=TM, n_rows=n_rows, do_fill=False)
    xb = _dispatch(h2_s, info_s, tile_tables(tiles_p, n_tiles) + fill, xb,
                   n_sub=TM // sub_s, sub_len=sub_s, n_rows=n_rows, do_fill=True)

    yb = _experts(xb, tb["ends"].astype(i32), w_up.reshape(N_EXPERTS, D_MODEL, 2 * D_FF),
                  b_up.reshape(N_EXPERTS, 1, 2 * D_FF), w_down.reshape(N_EXPERTS, D_FF, D_MODEL),
                  b_down.reshape(N_EXPERTS, 1, D_MODEL))

    g_post = row(g_post_ffn[0])
    y_p = _combine(yb, info_p, x1_p, mod_p, g_post, tile_tables(0, tiles_p), n_sub=1, sub_len=TM)
    y_s = _combine(yb, info_s, x1_s, mod_s, g_post, tile_tables(tiles_p, n_tiles),
                   n_sub=TM // sub_s, sub_len=sub_s)

    return (y_p, y_s, pool_p[None], pool_s[None], v_s.reshape(1, n_s, sub_s, GMLP_WIDTH))
```

```python
import functools

import jax
import jax.numpy as jnp
from jax import lax
from jax.experimental import pallas as pl
from jax.experimental.pallas import tpu as pltpu

D_MODEL = 1024
POOL_WINDOWS = (2, 4, 8, 16)
POOL_MAX = 16
POOL_GDIM = 128
POOL_WIDTH = 512
GMLP_HEADS = 4
GMLP_CHUNK = 128
GMLP_WIDTH = 512
N_EXPERTS = 32
TOP_K = 4
D_FF = 1024
SWIGLU_LIMIT = 7.0
SWIGLU_ALPHA = 1.702
NORM_EPS = 1e-6
LN_EPS = 1e-5
N_MOD = 6

TM = 256
ROW_ALIGN = 8
RLOC = 1280
BM = 512
EXPERT_CHAINS = 1
VMEM_LIMIT = 56 * 1024 * 1024

BF16 = jnp.bfloat16
F32 = jnp.float32


def _dot(a, b):
    return jnp.dot(a, b, preferred_element_type=F32)


def _rms(x, g):
    ms = jnp.mean(x * x, axis=-1, keepdims=True)
    return x * lax.rsqrt(ms + NORM_EPS) * g


def _sigmoid(x):
    return 0.5 * jnp.tanh(0.5 * x) + 0.5


def _gelu(x):
    return 0.5 * x * (1.0 + lax.erf(x * 0.7071067811865476))


def _ada_body(c_ref, w_ref, b_ref, o_ref):
    c = c_ref[...]
    s = c * jax.nn.sigmoid(c)
    o_ref[...] = _dot(s.astype(BF16), w_ref[...].astype(BF16)) + b_ref[...]


def _ada(c_all, w_ada, b_ada):
    n = c_all.shape[0]
    nt = N_MOD * D_MODEL // D_MODEL
    return pl.pallas_call(
        _ada_body,
        grid=(nt,),
        in_specs=[pl.BlockSpec((n, D_MODEL), lambda i: (0, 0)),
                  pl.BlockSpec((D_MODEL, D_MODEL), lambda i: (0, i)),
                  pl.BlockSpec((1, D_MODEL), lambda i: (0, i))],
        out_specs=pl.BlockSpec((n, D_MODEL), lambda i: (0, i)),
        out_shape=jax.ShapeDtypeStruct((n, N_MOD * D_MODEL), F32),
        compiler_params=pltpu.CompilerParams(dimension_semantics=("arbitrary",)),
        name="ada",
    )(c_all, w_ada, b_ada)


def _mixer_body(n_sub, sub_len, has_hist, pos0, *refs):
    it = iter(refs)
    x_ref = next(it)
    mod_ref = next(it)
    hist_ref = next(it) if has_hist else None
    (gpre_ref, win_ref, wpool_ref, spool_ref, gv_ref, bv_ref, ws_ref, bst_ref, pa_ref, pb_ref, wo_ref,
     gpost_ref, gffn_ref, wr_ref, br_ref) = [next(it) for _ in range(15)]
    x1_ref = next(it)
    h2_ref = next(it)
    pool_ref = next(it)
    v_ref = next(it) if has_hist else None
    info_ref = next(it)
    n8_ref = next(it)
    ext_ref = next(it)

    x = x_ref[...]
    mod = mod_ref[...]
    shift_m, scale_m, gate_m = mod[:, 0:1], mod[:, 1:2], mod[:, 2:3]
    shift_f, scale_f = mod[:, 3:4], mod[:, 4:5]

    h = _rms(x, gpre_ref[...]) * (1.0 + scale_m) + shift_m
    hb = h.reshape(TM, D_MODEL).astype(BF16)

    a = _dot(hb, win_ref[:, 0:POOL_WIDTH])
    if has_hist:
        for s in range(n_sub):
            ext_ref[s, 0:1, :] = jnp.zeros((1, POOL_WIDTH), F32)
            ext_ref[s, 1:POOL_MAX, :] = hist_ref[s]
            ext_ref[s, POOL_MAX:POOL_MAX + sub_len, :] = a[s * sub_len:(s + 1) * sub_len]
            pool_ref[s] = a[(s + 1) * sub_len - (POOL_MAX - 1):(s + 1) * sub_len]
        row0 = pos0
    else:
        j = pl.program_id(1)

        @pl.when(j == 0)
        def _():
            ext_ref[0, 0:POOL_MAX, :] = jnp.zeros((POOL_MAX, POOL_WIDTH), F32)

        @pl.when(j > 0)
        def _():
            ext_ref[0, 0:POOL_MAX, :] = ext_ref[0, sub_len:sub_len + POOL_MAX, :]

        ext_ref[0, POOL_MAX:POOL_MAX + sub_len, :] = a
        pool_ref[0] = a[sub_len - (POOL_MAX - 1):sub_len]
        row0 = pos0 + j * sub_len

    pos = row0 + lax.broadcasted_iota(jnp.int32, (sub_len, 1), 0)
    spool = spool_ref[...]
    ya_rows = []
    for s in range(n_sub):
        ya_groups = []
        for g, w in enumerate(POOL_WINDOWS):
            lanes = slice(g * POOL_GDIM, (g + 1) * POOL_GDIM)
            win = ext_ref[s, POOL_MAX:POOL_MAX + sub_len, lanes]
            for back in range(1, w):
                win = win + ext_ref[s, POOL_MAX - back:POOL_MAX - back + sub_len, lanes]
            cnt = jnp.minimum(pos + 1, w).astype(F32)
            dlt = win / cnt - a[s * sub_len:(s + 1) * sub_len, lanes]
            ya_groups.append(_dot(dlt.astype(BF16), wpool_ref[g]) * spool[:, lanes])
        ya_rows.append(jnp.concatenate(ya_groups, axis=1))
    y_a = jnp.concatenate(ya_rows, axis=0) if n_sub > 1 else ya_rows[0]

    u = _gelu(_dot(hb, win_ref[:, POOL_WIDTH:POOL_WIDTH + GMLP_WIDTH]))
    v = _gelu(_dot(hb, win_ref[:, POOL_WIDTH + GMLP_WIDTH:POOL_WIDTH + 2 * GMLP_WIDTH]))
    mu = jnp.mean(v, axis=-1, keepdims=True)
    vc = v - mu
    var = jnp.mean(vc * vc, axis=-1, keepdims=True)
    vn = vc * lax.rsqrt(var + LN_EPS) * gv_ref[...] + bv_ref[...]
    if has_hist:
        v_ref[...] = vn
    vb = vn.astype(BF16)
    seg = min(sub_len, GMLP_CHUNK)
    tri = (lax.broadcasted_iota(jnp.int32, (seg, seg), 0) >= lax.broadcasted_iota(jnp.int32, (seg, seg), 1))
    bst = bst_ref[...]
    yb_rows = []
    for c in range(TM // seg):
        rows = slice(c * seg, (c + 1) * seg)
        yb_heads = []
        for g in range(GMLP_HEADS):
            lanes = slice(g * 128, (g + 1) * 128)
            wm = jnp.where(tri, ws_ref[g, 0:seg, 0:seg], jnp.zeros((), BF16))
            sg = _dot(wm, vb[rows, lanes]) + bst[0:seg, g:g + 1]
            yb_heads.append(u[rows, lanes] * sg)
        yb_rows.append(jnp.concatenate(yb_heads, axis=1))
    y_b = jnp.concatenate(yb_rows, axis=0)

    c0 = POOL_WIDTH + 2 * GMLP_WIDTH
    ga = _dot(hb, win_ref[:, c0:c0 + D_MODEL])
    gb = _dot(hb, win_ref[:, c0 + D_MODEL:c0 + 2 * D_MODEL])
    m = (_sigmoid(ga) * _dot(y_a.astype(BF16), pa_ref[...])
         + _sigmoid(gb) * _dot(y_b.astype(BF16), pb_ref[...]))
    mo = _dot(m.astype(BF16), wo_ref[...]).reshape(n_sub, sub_len, D_MODEL)
    x1 = x + gate_m * _rms(mo, gpost_ref[...])
    x1_ref[...] = x1

    h2 = (_rms(x1, gffn_ref[...]) * (1.0 + scale_f) + shift_f).astype(BF16)
    h2_ref[...] = h2
    logits = lax.dot_general(wr_ref[...], h2.reshape(TM, D_MODEL), (((1,), (1,)), ((), ())),
                             preferred_element_type=F32) + br_ref[...]

    erow = lax.broadcasted_iota(jnp.int32, (N_EXPERTS, TM), 0)
    work = logits
    onehots, vals = [], []
    for _ in range(TOP_K):
        mx = jnp.max(work, axis=0, keepdims=True)
        idx = jnp.min(jnp.where(work == mx, erow, N_EXPERTS), axis=0, keepdims=True)
        oh = erow == idx
        onehots.append(oh)
        vals.append(mx)
        work = jnp.where(oh, -jnp.inf, work)
    exps = [jnp.exp(vk - vals[0]) for vk in vals]
    den = exps[0] + exps[1] + exps[2] + exps[3]
    gates = [ek / den for ek in exps]

    sel = jnp.zeros((N_EXPERTS, TM), F32)
    for oh in onehots:
        sel = sel + oh.astype(F32)
    before = (lax.broadcasted_iota(jnp.int32, (TM, TM), 0) < lax.broadcasted_iota(jnp.int32, (TM, TM), 1))
    rank = _dot(sel.astype(BF16), before.astype(BF16))
    n_e = jnp.sum(sel, axis=1, keepdims=True).astype(jnp.int32)
    n8 = ((n_e + (ROW_ALIGN - 1)) // ROW_ALIGN) * ROW_ALIGN
    lower = (lax.broadcasted_iota(jnp.int32, (N_EXPERTS, N_EXPERTS), 0)
             > lax.broadcasted_iota(jnp.int32, (N_EXPERTS, N_EXPERTS), 1))
    n8b = jnp.broadcast_to(n8.astype(F32), (N_EXPERTS, 128)).astype(BF16)
    off = _dot(lower.astype(BF16), n8b)[:, 0:1]
    slot = off + rank
    irow = lax.broadcasted_iota(jnp.int32, (2 * TOP_K, TM), 0)
    info = jnp.zeros((2 * TOP_K, TM), F32)
    for k in range(TOP_K):
        p_k = jnp.sum(jnp.where(onehots[k], slot, 0.0), axis=0, keepdims=True)
        info = info + jnp.where(irow == k, p_k, 0.0) + jnp.where(irow == TOP_K + k, gates[k], 0.0)
    info_ref[...] = info
    n8_ref[...] = n8.reshape(1, N_EXPERTS, 1)


def _mixer(x, mod, hist, weights, *, n_sub, sub_len, pos0):
    n_seq, seq_len, _ = x.shape
    has_hist = hist is not None
    tiles_per_seq = seq_len // sub_len if not has_hist else 1
    if has_hist:
        assert seq_len == sub_len and n_seq % n_sub == 0
        grid = (n_seq // n_sub, 1)
        n_tiles = n_seq // n_sub
        xmap = lambda i, j: (i, 0, 0)
        tile = lambda i, j: i
    else:
        assert n_sub == 1 and seq_len % sub_len == 0
        grid = (n_seq, tiles_per_seq)
        n_tiles = n_seq * tiles_per_seq
        xmap = lambda i, j: (i, j, 0)
        tile = lambda i, j: i * tiles_per_seq + j
    assert n_sub * sub_len == TM

    def full(arr):
        nd = arr.ndim
        return pl.BlockSpec(arr.shape, lambda i, j, _nd=nd: (0,) * _nd)

    in_specs = [pl.BlockSpec((n_sub, sub_len, D_MODEL), xmap),
                pl.BlockSpec((n_sub, N_MOD, D_MODEL), lambda i, j: (i, 0, 0))]
    args = [x, mod]
    if has_hist:
        in_specs.append(pl.BlockSpec((n_sub, POOL_MAX - 1, POOL_WIDTH), lambda i, j: (i, 0, 0)))
        args.append(hist)
    in_specs += [full(w) for w in weights]
    args += list(weights)

    out_shape = [jax.ShapeDtypeStruct(x.shape, F32),
                 jax.ShapeDtypeStruct(x.shape, BF16),
                 jax.ShapeDtypeStruct((n_seq, POOL_MAX - 1, POOL_WIDTH), F32)]
    out_specs = [pl.BlockSpec((n_sub, sub_len, D_MODEL), xmap),
                 pl.BlockSpec((n_sub, sub_len, D_MODEL), xmap),
                 pl.BlockSpec((n_sub, POOL_MAX - 1, POOL_WIDTH), lambda i, j: (i, 0, 0))]
    if has_hist:
        out_shape.append(jax.ShapeDtypeStruct((n_seq * seq_len, GMLP_WIDTH), F32))
        out_specs.append(pl.BlockSpec((TM, GMLP_WIDTH), lambda i, j: (i, 0)))
    out_shape += [jax.ShapeDtypeStruct((n_tiles * 2 * TOP_K, TM), F32),
                  jax.ShapeDtypeStruct((n_tiles, N_EXPERTS, 1), jnp.int32)]
    out_specs += [pl.BlockSpec((2 * TOP_K, TM), lambda i, j: (tile(i, j), 0)),
                  pl.BlockSpec((1, N_EXPERTS, 1), lambda i, j: (tile(i, j), 0, 0))]

    return pl.pallas_call(
        functools.partial(_mixer_body, n_sub, sub_len, has_hist, pos0),
        grid=grid,
        in_specs=in_specs,
        out_specs=out_specs,
        out_shape=out_shape,
        scratch_shapes=[pltpu.VMEM((n_sub, POOL_MAX + sub_len, POOL_WIDTH), F32)],
        compiler_params=pltpu.CompilerParams(dimension_semantics=("arbitrary", "arbitrary"),
                                             vmem_limit_bytes=VMEM_LIMIT),
        name="mixer_hist" if has_hist else "mixer",
    )(*args)


def _slot_matrix(info, weights):
    rows = lax.broadcasted_iota(jnp.int32, (RLOC, TM), 0)
    out = jnp.zeros((RLOC, TM), F32)
    for k in range(TOP_K):
        p_k = info[k:k + 1, :].astype(jnp.int32)
        out = out + jnp.where(p_k == rows, weights[k], 0.0)
    return out.astype(BF16)


def _run_copy(src_ref, src_row, dst_ref, dst_row, n_rows, sem):
    return pltpu.make_async_copy(src_ref.at[pl.ds(src_row, n_rows), :],
                                 dst_ref.at[pl.ds(dst_row, n_rows), :], sem)


def _dispatch_body(n_tiles, do_fill, n8_s, off_s, base_s, tot_s, fill_lo_s, fill_n_s, *refs):
    if do_fill:
        h2_ref, info_ref, _xb_in, xb_ref, xs_ref, zero_ref, sems = refs
    else:
        h2_ref, info_ref, xb_ref, xs_ref, zero_ref, sems = refs
    i = pl.program_id(0)
    slot = i % 2

    def wait_tile(t, sl):
        tot = pl.multiple_of(tot_s[t], ROW_ALIGN)
        _run_copy(xs_ref.at[sl], 0, xb_ref, 0, tot, sems.at[sl]).wait()

    @pl.when(i >= 2)
    def _():
        wait_tile(i - 2, slot)

    info = info_ref[...]
    ones = [1.0] * TOP_K
    perm = _slot_matrix(info, ones)
    xs_ref[slot] = _dot(perm, h2_ref[...].reshape(TM, D_MODEL))
    for e in range(N_EXPERTS):
        n = pl.multiple_of(n8_s[i * N_EXPERTS + e], ROW_ALIGN)

        @pl.when(n > 0)
        def _(e=e, n=n):
            src = pl.multiple_of(off_s[i * N_EXPERTS + e], ROW_ALIGN)
            dst = pl.multiple_of(base_s[i * N_EXPERTS + e], ROW_ALIGN)
            _run_copy(xs_ref.at[slot], src, xb_ref, dst, n, sems.at[slot]).start()

    @pl.when(i == n_tiles - 1)
    def _():
        if n_tiles >= 2:
            wait_tile(i - 1, 1 - slot)
        wait_tile(i, slot)
        if do_fill:
            zero_ref[...] = jnp.zeros(zero_ref.shape, F32)
            for e in range(N_EXPERTS):
                n = pl.multiple_of(fill_n_s[e], ROW_ALIGN)

                @pl.when(n > 0)
                def _(e=e, n=n):
                    dst = pl.multiple_of(fill_lo_s[e], ROW_ALIGN)
                    cp = _run_copy(zero_ref, 0, xb_ref, dst, n, sems.at[2])
                    cp.start()
                    cp.wait()


def _dispatch(h2, info, tables, xb, *, n_sub, sub_len, n_rows, do_fill):
    n_seq, seq_len, _ = h2.shape
    tiles_per_seq = seq_len // sub_len
    n_tiles = n_seq // n_sub * tiles_per_seq
    if n_sub == 1:
        hmap = lambda i, *_: (i // tiles_per_seq, i % tiles_per_seq, 0)
    else:
        hmap = lambda i, *_: (i, 0, 0)
    in_specs = [pl.BlockSpec((n_sub, sub_len, D_MODEL), hmap),
                pl.BlockSpec((2 * TOP_K, TM), lambda i, *_: (i, 0))]
    args = [h2, info]
    aliases = {}
    if xb is not None:
        in_specs.append(pl.BlockSpec(memory_space=pl.ANY))
        args.append(xb)
        aliases = {len(tables) + 2: 0}
    assert (xb is not None) == do_fill
    return pl.pallas_call(
        functools.partial(_dispatch_body, n_tiles, do_fill),
        grid_spec=pltpu.PrefetchScalarGridSpec(
            num_scalar_prefetch=len(tables),
            grid=(n_tiles,),
            in_specs=in_specs,
            out_specs=pl.BlockSpec(memory_space=pl.ANY),
            scratch_shapes=[pltpu.VMEM((2, RLOC, D_MODEL), F32),
                            pltpu.VMEM((BM, D_MODEL), F32),
                            pltpu.SemaphoreType.DMA((3,))],
        ),
        out_shape=jax.ShapeDtypeStruct((n_rows, D_MODEL), F32),
        input_output_aliases=aliases,
        compiler_params=pltpu.CompilerParams(dimension_semantics=("arbitrary",),
                                             vmem_limit_bytes=VMEM_LIMIT),
        name="dispatch_fill" if do_fill else "dispatch",
    )(*tables, *args)


def _expert_rows(xv, wup_bf, bup, wdn_bf, bdn):
    z = _dot(xv.astype(BF16), wup_bf[...]) + bup
    zg = jnp.minimum(z[:, :D_FF], SWIGLU_LIMIT)
    zl = jnp.clip(z[:, D_FF:], -SWIGLU_LIMIT, SWIGLU_LIMIT)
    act = zg * _sigmoid(SWIGLU_ALPHA * zg) * (zl + 1.0)
    return _dot(act.astype(BF16), wdn_bf[...]) + bdn


def _expert_body(ends_s, xb_ref, wup_ref, bup_ref, wdn_ref, bdn_ref, yb_ref, x_buf, y_buf, wup_bf, wdn_bf,
                 x_sem, y_sem):
    e = pl.program_id(0)
    n_total = ends_s[N_EXPERTS - 1]
    lo = jnp.where(e == 0, 0, ends_s[jnp.maximum(e - 1, 0)])
    hi = ends_s[e]

    def x_copy(blk, slot):
        return _run_copy(xb_ref, pl.multiple_of(blk * BM, BM), x_buf.at[slot], 0, BM, x_sem.at[slot])

    def y_copy(blk, slot):
        return _run_copy(y_buf.at[slot], 0, yb_ref, pl.multiple_of(blk * BM, BM), BM, y_sem.at[slot])

    @pl.when(e == 0)
    def _():
        x_copy(0, 0).start()

    @pl.when(hi > lo)
    def _():
        wup_bf[...] = wup_ref[...].astype(BF16)
        wdn_bf[...] = wdn_ref[...].astype(BF16)
        bup = bup_ref[...]
        bdn = bdn_ref[...]

        def block(blk, carry):
            slot = blk % 2
            x_copy(blk, slot).wait()

            @pl.when(blk + 1 < n_total)
            def _():
                x_copy(blk + 1, 1 - slot).start()

            @pl.when(blk >= 2)
            def _():
                y_copy(blk - 2, slot).wait()

            sub = BM // EXPERT_CHAINS
            for c in range(EXPERT_CHAINS):
                rows = pl.ds(c * sub, sub)
                y_buf[slot, rows, :] = _expert_rows(x_buf[slot, rows, :], wup_bf, bup, wdn_bf, bdn)
            y_copy(blk, slot).start()
            return carry

        lax.fori_loop(lo, hi, block, 0)

    @pl.when(e == N_EXPERTS - 1)
    def _():
        y_copy(n_total - 2, n_total % 2).wait()
        y_copy(n_total - 1, (n_total - 1) % 2).wait()


def _experts(xb, ends, w_up, b_up, w_down, b_down):
    n_rows = xb.shape[0]
    wmap = lambda e, ends: (e, 0, 0)
    return pl.pallas_call(
        _expert_body,
        grid_spec=pltpu.PrefetchScalarGridSpec(
            num_scalar_prefetch=1,
            grid=(N_EXPERTS,),
            in_specs=[pl.BlockSpec(memory_space=pl.ANY),
                      pl.BlockSpec((None, D_MODEL, 2 * D_FF), wmap),
                      pl.BlockSpec((None, 1, 2 * D_FF), wmap),
                      pl.BlockSpec((None, D_FF, D_MODEL), wmap),
                      pl.BlockSpec((None, 1, D_MODEL), wmap)],
            out_specs=pl.BlockSpec(memory_space=pl.ANY),
            scratch_shapes=[pltpu.VMEM((2, BM, D_MODEL), F32),
                            pltpu.VMEM((2, BM, D_MODEL), F32),
                            pltpu.VMEM((D_MODEL, 2 * D_FF), BF16),
                            pltpu.VMEM((D_FF, D_MODEL), BF16),
                            pltpu.SemaphoreType.DMA((2,)),
                            pltpu.SemaphoreType.DMA((2,))],
        ),
        out_shape=jax.ShapeDtypeStruct((n_rows, D_MODEL), F32),
        compiler_params=pltpu.CompilerParams(dimension_semantics=("arbitrary",),
                                             vmem_limit_bytes=VMEM_LIMIT),
        name="experts",
    )(ends, xb, w_up, b_up, w_down, b_down)


def _combine_body(n_tiles, n8_s, off_s, base_s, tot_s, yb_ref, info_ref, x1_ref, mod_ref, gpost_ref, o_ref,
                  yloc_ref, sems):
    i = pl.program_id(0)
    slot = i % 2

    def start_tile(t, sl):
        for e in range(N_EXPERTS):
            n = pl.multiple_of(n8_s[t * N_EXPERTS + e], ROW_ALIGN)

            @pl.when(n > 0)
            def _(e=e, n=n):
                src = pl.multiple_of(base_s[t * N_EXPERTS + e], ROW_ALIGN)
                dst = pl.multiple_of(off_s[t * N_EXPERTS + e], ROW_ALIGN)
                _run_copy(yb_ref, src, yloc_ref.at[sl], dst, n, sems.at[sl]).start()

    @pl.when(i == 0)
    def _():
        yloc_ref[:, TM * TOP_K:RLOC, :] = jnp.zeros((2, RLOC - TM * TOP_K, D_MODEL), F32)
        start_tile(0, 0)

    @pl.when(i + 1 < n_tiles)
    def _():
        start_tile(i + 1, 1 - slot)

    tot = pl.multiple_of(tot_s[i], ROW_ALIGN)
    _run_copy(yb_ref, 0, yloc_ref.at[slot], 0, tot, sems.at[slot]).wait()

    info = info_ref[...]
    wmat = _slot_matrix(info, [info[TOP_K + k:TOP_K + k + 1, :] for k in range(TOP_K)])
    f = lax.dot_general(wmat, yloc_ref[slot].astype(BF16), (((0,), (0,)), ((), ())),
                        preferred_element_type=F32)
    x1 = x1_ref[...]
    gate_f = mod_ref[...][:, 5:6]
    o_ref[...] = x1 + gate_f * _rms(f.reshape(x1.shape), gpost_ref[...])


def _combine(yb, info, x1, mod, g_post, tables, *, n_sub, sub_len):
    n_seq, seq_len, _ = x1.shape
    tiles_per_seq = seq_len // sub_len
    n_tiles = n_seq // n_sub * tiles_per_seq
    if n_sub == 1:
        xmap = lambda i, *_: (i // tiles_per_seq, i % tiles_per_seq, 0)
        mmap = lambda i, *_: (i // tiles_per_seq, 0, 0)
    else:
        xmap = lambda i, *_: (i, 0, 0)
        mmap = lambda i, *_: (i, 0, 0)
    return pl.pallas_call(
        functools.partial(_combine_body, n_tiles),
        grid_spec=pltpu.PrefetchScalarGridSpec(
            num_scalar_prefetch=len(tables),
            grid=(n_tiles,),
            in_specs=[pl.BlockSpec(memory_space=pl.ANY),
                      pl.BlockSpec((2 * TOP_K, TM), lambda i, *_: (i, 0)),
                      pl.BlockSpec((n_sub, sub_len, D_MODEL), xmap),
                      pl.BlockSpec((n_sub, N_MOD, D_MODEL), mmap),
                      pl.BlockSpec((1, D_MODEL), lambda i, *_: (0, 0))],
            out_specs=pl.BlockSpec((n_sub, sub_len, D_MODEL), xmap),
            scratch_shapes=[pltpu.VMEM((2, RLOC, D_MODEL), F32),
                            pltpu.SemaphoreType.DMA((2,))],
        ),
        out_shape=jax.ShapeDtypeStruct(x1.shape, F32),
        compiler_params=pltpu.CompilerParams(dimension_semantics=("arbitrary",),
                                             vmem_limit_bytes=VMEM_LIMIT),
        name="combine",
    )(*tables, yb, info, x1, mod, g_post)


def _tables(n8_all):
    n_tiles = n8_all.shape[0]
    off = jnp.cumsum(n8_all, axis=1) - n8_all
    tot = jnp.sum(n8_all, axis=1)
    cnt = jnp.sum(n8_all, axis=0)
    region = (cnt + BM - 1) // BM * BM
    start = jnp.cumsum(region) - region
    base = start[None, :] + jnp.cumsum(n8_all, axis=0) - n8_all
    fill_lo = start + cnt
    fill_n = region - cnt
    ends = jnp.cumsum(region // BM)
    nblk = ends[-1]
    return dict(n8=n8_all, off=off, base=base, tot=tot, fill_lo=fill_lo, fill_n=fill_n, ends=ends, nblk=nblk)


def _max_rows(n_tokens, n_tiles):
    worst = n_tokens * TOP_K + n_tiles * N_EXPERTS * (ROW_ALIGN - 1) + N_EXPERTS * (BM - ROW_ALIGN)
    return (worst + BM - 1) // BM * BM


def kernel(x_prompt, x_sample, cache_pool, c_prompt, c_sample, w_ada, b_ada, g_pre_mix, g_post_mix, g_pre_ffn, g_post_ffn, w_in, w_pool, s_pool, g_v, b_v, w_s, b_s, p_a, p_b, w_o, w_router, b_router, w_up, b_up, w_down, b_down):
    assert w_ada.shape[0] == 1, "single-layer trunk"
    n_p, n_s = x_prompt.shape[0], x_sample.shape[0]
    past_len = x_prompt.shape[1]
    i32 = jnp.int32

    mod = _ada(jnp.concatenate([c_prompt, c_sample], axis=0), w_ada.reshape(D_MODEL, N_MOD * D_MODEL), b_ada)
    mod = mod.reshape(n_p + n_s, N_MOD, D_MODEL)
    mod_p, mod_s = mod[:n_p], mod[n_p:]

    row = lambda v: v.reshape(1, -1)
    weights = (row(g_pre_mix[0]), w_in[0].astype(BF16), w_pool[0].astype(BF16), row(s_pool[0]), row(g_v[0]),
               row(b_v[0]), w_s[0].astype(BF16), b_s[0].T, p_a[0].astype(BF16), p_b[0].astype(BF16),
               w_o[0].astype(BF16), row(g_post_mix[0]), row(g_pre_ffn[0]), w_router[0].T.astype(BF16),
               b_router[0].reshape(-1, 1))

    sub_s = x_sample.shape[1]
    x1_p, h2_p, pool_p, info_p, n8_p = _mixer(x_prompt, mod_p, None, weights, n_sub=1, sub_len=TM, pos0=0)
    x1_s, h2_s, pool_s, v_s, info_s, n8_s = _mixer(x_sample, mod_s, cache_pool[0], weights,
                                                   n_sub=TM // sub_s, sub_len=sub_s, pos0=past_len)

    tiles_p, tiles_s = n8_p.shape[0], n8_s.shape[0]
    n_tiles = tiles_p + tiles_s
    tb = _tables(jnp.concatenate([n8_p, n8_s], axis=0).reshape(n_tiles, N_EXPERTS))
    n_rows = _max_rows(n_tiles * TM, n_tiles)
    n_blocks = n_rows // BM

    def tile_tables(lo, hi):
        return (tb["n8"][lo:hi].reshape(-1).astype(i32), tb["off"][lo:hi].reshape(-1).astype(i32),
                tb["base"][lo:hi].reshape(-1).astype(i32), tb["tot"][lo:hi].astype(i32))

    fill = (tb["fill_lo"].astype(i32), tb["fill_n"].astype(i32))
    xb = _dispatch(h2_p, info_p, tile_tables(0, tiles_p) + fill, None,
                   n_sub=1, sub_len=TM, n_rows=n_rows, do_fill=False)
    xb = _dispatch(h2_s, info_s, tile_tables(tiles_p, n_tiles) + fill, xb,
                   n_sub=TM // sub_s, sub_len=sub_s, n_rows=n_rows, do_fill=True)

    yb = _experts(xb, tb["ends"].astype(i32), w_up.reshape(N_EXPERTS, D_MODEL, 2 * D_FF),
                  b_up.reshape(N_EXPERTS, 1, 2 * D_FF), w_down.reshape(N_EXPERTS, D_FF, D_MODEL),
                  b_down.reshape(N_EXPERTS, 1, D_MODEL))

    g_post = row(g_post_ffn[0])
    y_p = _combine(yb, info_p, x1_p, mod_p, g_post, tile_tables(0, tiles_p), n_sub=1, sub_len=TM)
    y_s = _combine(yb, info_s, x1_s, mod_s, g_post, tile_tables(tiles_p, n_tiles),
                   n_sub=TM // sub_s, sub_len=sub_s)

    return (y_p, y_s, pool_p[None], pool_s[None], v_s.reshape(1, n_s, sub_s, GMLP_WIDTH))
```

```python
import functools

import jax
import jax.numpy as jnp
from jax import lax
from jax.experimental import pallas as pl
from jax.experimental.pallas import tpu as pltpu

D_MODEL = 1024
POOL_WINDOWS = (2, 4, 8, 16)
POOL_MAX = 16
POOL_GDIM = 128
POOL_WIDTH = 512
GMLP_HEADS = 4
GMLP_CHUNK = 128
GMLP_WIDTH = 512
N_EXPERTS = 32
TOP_K = 4
D_FF = 1024
SWIGLU_LIMIT = 7.0
SWIGLU_ALPHA = 1.702
NORM_EPS = 1e-6
LN_EPS = 1e-5
N_MOD = 6

TM = 256
ROW_ALIGN = 8
RLOC = 1280
BM = 512
EXPERT_CHAINS = 1
MIXER_CHAINS = 4
VMEM_LIMIT = 56 * 1024 * 1024

BF16 = jnp.bfloat16
F32 = jnp.float32


def _dot(a, b):
    return jnp.dot(a, b, preferred_element_type=F32)


def _rms(x, g):
    ms = jnp.mean(x * x, axis=-1, keepdims=True)
    return x * lax.rsqrt(ms + NORM_EPS) * g


def _sigmoid(x):
    return 0.5 * jnp.tanh(0.5 * x) + 0.5


def _gelu(x):
    return 0.5 * x * (1.0 + lax.erf(x * 0.7071067811865476))


def _ada_body(c_ref, w_ref, b_ref, o_ref):
    c = c_ref[...]
    s = c * jax.nn.sigmoid(c)
    o_ref[...] = _dot(s.astype(BF16), w_ref[...].astype(BF16)) + b_ref[...]


def _ada(c_all, w_ada, b_ada):
    n = c_all.shape[0]
    nt = N_MOD * D_MODEL // D_MODEL
    return pl.pallas_call(
        _ada_body,
        grid=(nt,),
        in_specs=[pl.BlockSpec((n, D_MODEL), lambda i: (0, 0)),
                  pl.BlockSpec((D_MODEL, D_MODEL), lambda i: (0, i)),
                  pl.BlockSpec((1, D_MODEL), lambda i: (0, i))],
        out_specs=pl.BlockSpec((n, D_MODEL), lambda i: (0, i)),
        out_shape=jax.ShapeDtypeStruct((n, N_MOD * D_MODEL), F32),
        compiler_params=pltpu.CompilerParams(dimension_semantics=("arbitrary",)),
        name="ada",
    )(c_all, w_ada, b_ada)


def _mixer_body(n_chain, n_sub, sub_len, has_hist, pos0, *refs):
    tiles = [_mixer_tile(c, n_chain, n_sub, sub_len, has_hist, pos0, *refs) for c in range(n_chain)]
    live = [True] * n_chain
    t = 0
    while any(live):
        for c, tile in enumerate(tiles):
            if live[c] and t >= c:
                live[c] = next(tile, None) is not None
        t += 1


def _mixer_tile(c, n_chain, n_sub, sub_len, has_hist, pos0, *refs):
    it = iter(refs)
    x_ref = next(it)
    mod_ref = next(it)
    hist_ref = next(it) if has_hist else None
    (gpre_ref, win_ref, wpool_ref, spool_ref, gv_ref, bv_ref, ws_ref, bst_ref, pa_ref, pb_ref, wo_ref,
     gpost_ref, gffn_ref, wr_ref, br_ref) = [next(it) for _ in range(15)]
    x1_ref = next(it)
    h2_ref = next(it)
    pool_ref = next(it)
    v_ref = next(it) if has_hist else None
    info_ref = next(it)
    n8_ref = next(it)
    ext_ref = next(it)

    if has_hist:
        seqs, rows = slice(c * n_sub, (c + 1) * n_sub), slice(None)
        mod = mod_ref[seqs]
    else:
        seqs, rows = slice(None), slice(c * sub_len, (c + 1) * sub_len)
        mod = mod_ref[...]
    x = x_ref[seqs, rows, :]
    shift_m, scale_m, gate_m = mod[:, 0:1], mod[:, 1:2], mod[:, 2:3]
    shift_f, scale_f = mod[:, 3:4], mod[:, 4:5]

    h = _rms(x, gpre_ref[...]) * (1.0 + scale_m) + shift_m
    hb = h.reshape(TM, D_MODEL).astype(BF16)

    a = _dot(hb, win_ref[:, 0:POOL_WIDTH])
    yield True
    e0 = c * n_sub
    if has_hist:
        for s in range(n_sub):
            ext_ref[e0 + s, 0:1, :] = jnp.zeros((1, POOL_WIDTH), F32)
            ext_ref[e0 + s, 1:POOL_MAX, :] = hist_ref[e0 + s]
            ext_ref[e0 + s, POOL_MAX:POOL_MAX + sub_len, :] = a[s * sub_len:(s + 1) * sub_len]
            pool_ref[e0 + s] = a[(s + 1) * sub_len - (POOL_MAX - 1):(s + 1) * sub_len]
        row0 = pos0
    else:
        j = pl.program_id(1)
        prev = (c - 1) % n_chain
        if c == 0:
            @pl.when(j == 0)
            def _():
                ext_ref[0, 0:POOL_MAX, :] = jnp.zeros((POOL_MAX, POOL_WIDTH), F32)

            @pl.when(j > 0)
            def _():
                ext_ref[0, 0:POOL_MAX, :] = ext_ref[prev, sub_len:sub_len + POOL_MAX, :]
        else:
            ext_ref[c, 0:POOL_MAX, :] = ext_ref[prev, sub_len:sub_len + POOL_MAX, :]

        ext_ref[c, POOL_MAX:POOL_MAX + sub_len, :] = a
        if c == n_chain - 1:
            pool_ref[0] = a[sub_len - (POOL_MAX - 1):sub_len]
        row0 = pos0 + (j * n_chain + c) * sub_len

    pos = row0 + lax.broadcasted_iota(jnp.int32, (sub_len, 1), 0)
    spool = spool_ref[...]
    ya_rows = []
    for s in range(n_sub):
        ya_groups = []
        for g, w in enumerate(POOL_WINDOWS):
            lanes = slice(g * POOL_GDIM, (g + 1) * POOL_GDIM)
            win = ext_ref[e0 + s, POOL_MAX:POOL_MAX + sub_len, lanes]
            for back in range(1, w):
                win = win + ext_ref[e0 + s, POOL_MAX - back:POOL_MAX - back + sub_len, lanes]
            cnt = jnp.minimum(pos + 1, w).astype(F32)
            dlt = win / cnt - a[s * sub_len:(s + 1) * sub_len, lanes]
            ya_groups.append(_dot(dlt.astype(BF16), wpool_ref[g]) * spool[:, lanes])
        ya_rows.append(jnp.concatenate(ya_groups, axis=1))
    y_a = jnp.concatenate(ya_rows, axis=0) if n_sub > 1 else ya_rows[0]
    yield True

    u = _dot(hb, win_ref[:, POOL_WIDTH:POOL_WIDTH + GMLP_WIDTH])
    v = _dot(hb, win_ref[:, POOL_WIDTH + GMLP_WIDTH:POOL_WIDTH + 2 * GMLP_WIDTH])
    yield True
    u = _gelu(u)
    v = _gelu(v)
    mu = jnp.mean(v, axis=-1, keepdims=True)
    vc = v - mu
    var = jnp.mean(vc * vc, axis=-1, keepdims=True)
    vn = vc * lax.rsqrt(var + LN_EPS) * gv_ref[...] + bv_ref[...]
    if has_hist:
        v_ref[c * TM:(c + 1) * TM, :] = vn
    vb = vn.astype(BF16)
    seg = min(sub_len, GMLP_CHUNK)
    tri = (lax.broadcasted_iota(jnp.int32, (seg, seg), 0) >= lax.broadcasted_iota(jnp.int32, (seg, seg), 1))
    bst = bst_ref[...]
    yb_rows = []
    for q in range(TM // seg):
        srows = slice(q * seg, (q + 1) * seg)
        yb_heads = []
        for g in range(GMLP_HEADS):
            lanes = slice(g * 128, (g + 1) * 128)
            wm = jnp.where(tri, ws_ref[g, 0:seg, 0:seg], jnp.zeros((), BF16))
            sg = _dot(wm, vb[srows, lanes]) + bst[0:seg, g:g + 1]
            yb_heads.append(u[srows, lanes] * sg)
        yb_rows.append(jnp.concatenate(yb_heads, axis=1))
    y_b = jnp.concatenate(yb_rows, axis=0)
    yield True

    c0 = POOL_WIDTH + 2 * GMLP_WIDTH
    ga = _dot(hb, win_ref[:, c0:c0 + D_MODEL])
    gb = _dot(hb, win_ref[:, c0 + D_MODEL:c0 + 2 * D_MODEL])
    ma = _dot(y_a.astype(BF16), pa_ref[...])
    mb = _dot(y_b.astype(BF16), pb_ref[...])
    yield True
    m = _sigmoid(ga) * ma + _sigmoid(gb) * mb
    yield True
    mo = _dot(m.astype(BF16), wo_ref[...]).reshape(n_sub, sub_len, D_MODEL)
    yield True
    x1 = x + gate_m * _rms(mo, gpost_ref[...])
    x1_ref[seqs, rows, :] = x1

    h2 = (_rms(x1, gffn_ref[...]) * (1.0 + scale_f) + shift_f).astype(BF16)
    h2_ref[seqs, rows, :] = h2
    yield True
    logits = lax.dot_general(wr_ref[...], h2.reshape(TM, D_MODEL), (((1,), (1,)), ((), ())),
                             preferred_element_type=F32) + br_ref[...]

    erow = lax.broadcasted_iota(jnp.int32, (N_EXPERTS, TM), 0)
    work = logits
    onehots, vals = [], []
    for _ in range(TOP_K):
        mx = jnp.max(work, axis=0, keepdims=True)
        idx = jnp.min(jnp.where(work == mx, erow, N_EXPERTS), axis=0, keepdims=True)
        oh = erow == idx
        onehots.append(oh)
        vals.append(mx)
        work = jnp.where(oh, -jnp.inf, work)
    exps = [jnp.exp(vk - vals[0]) for vk in vals]
    den = exps[0] + exps[1] + exps[2] + exps[3]
    gates = [ek / den for ek in exps]

    sel = jnp.zeros((N_EXPERTS, TM), F32)
    for oh in onehots:
        sel = sel + oh.astype(F32)
    before = (lax.broadcasted_iota(jnp.int32, (TM, TM), 0) < lax.broadcasted_iota(jnp.int32, (TM, TM), 1))
    rank = _dot(sel.astype(BF16), before.astype(BF16))
    n_e = jnp.sum(sel, axis=1, keepdims=True).astype(jnp.int32)
    n8 = ((n_e + (ROW_ALIGN - 1)) // ROW_ALIGN) * ROW_ALIGN
    lower = (lax.broadcasted_iota(jnp.int32, (N_EXPERTS, N_EXPERTS), 0)
             > lax.broadcasted_iota(jnp.int32, (N_EXPERTS, N_EXPERTS), 1))
    n8b = jnp.broadcast_to(n8.astype(F32), (N_EXPERTS, 128)).astype(BF16)
    off = _dot(lower.astype(BF16), n8b)[:, 0:1]
    slot = off + rank
    irow = lax.broadcasted_iota(jnp.int32, (2 * TOP_K, TM), 0)
    info = jnp.zeros((2 * TOP_K, TM), F32)
    for k in range(TOP_K):
        p_k = jnp.sum(jnp.where(onehots[k], slot, 0.0), axis=0, keepdims=True)
        info = info + jnp.where(irow == k, p_k, 0.0) + jnp.where(irow == TOP_K + k, gates[k], 0.0)
    info_ref[c * 2 * TOP_K:(c + 1) * 2 * TOP_K, :] = info
    n8_ref[c] = n8


def _mixer(x, mod, hist, weights, *, n_sub, sub_len, pos0):
    n_seq, seq_len, _ = x.shape
    has_hist = hist is not None
    n_chain = MIXER_CHAINS
    assert n_sub * sub_len == TM
    if has_hist:
        assert seq_len == sub_len and n_seq % (n_sub * n_chain) == 0
        blk_seq, blk_rows = n_sub * n_chain, sub_len
        grid = (n_seq // blk_seq, 1)
        xmap = lambda i, j: (i, 0, 0)
        step = lambda i, j: i
    else:
        assert n_sub == 1 and seq_len % (sub_len * n_chain) == 0
        blk_seq, blk_rows = 1, sub_len * n_chain
        steps_per_seq = seq_len // blk_rows
        grid = (n_seq, steps_per_seq)
        xmap = lambda i, j: (i, j, 0)
        step = lambda i, j: i * steps_per_seq + j
    n_tiles = grid[0] * grid[1] * n_chain

    def full(arr):
        nd = arr.ndim
        return pl.BlockSpec(arr.shape, lambda i, j, _nd=nd: (0,) * _nd, pipeline_mode=pl.Buffered(1))

    in_specs = [pl.BlockSpec((blk_seq, blk_rows, D_MODEL), xmap),
                pl.BlockSpec((blk_seq, N_MOD, D_MODEL), lambda i, j: (i, 0, 0))]
    args = [x, mod]
    if has_hist:
        in_specs.append(pl.BlockSpec((blk_seq, POOL_MAX - 1, POOL_WIDTH), lambda i, j: (i, 0, 0)))
        args.append(hist)
    in_specs += [full(w) for w in weights]
    args += list(weights)

    out_shape = [jax.ShapeDtypeStruct(x.shape, F32),
                 jax.ShapeDtypeStruct(x.shape, BF16),
                 jax.ShapeDtypeStruct((n_seq, POOL_MAX - 1, POOL_WIDTH), F32)]
    out_specs = [pl.BlockSpec((blk_seq, blk_rows, D_MODEL), xmap),
                 pl.BlockSpec((blk_seq, blk_rows, D_MODEL), xmap),
                 pl.BlockSpec((blk_seq, POOL_MAX - 1, POOL_WIDTH), lambda i, j: (i, 0, 0))]
    if has_hist:
        out_shape.append(jax.ShapeDtypeStruct((n_seq * seq_len, GMLP_WIDTH), F32))
        out_specs.append(pl.BlockSpec((n_chain * TM, GMLP_WIDTH), lambda i, j: (i, 0)))
    out_shape += [jax.ShapeDtypeStruct((n_tiles * 2 * TOP_K, TM), F32),
                  jax.ShapeDtypeStruct((n_tiles, N_EXPERTS, 1), jnp.int32)]
    out_specs += [pl.BlockSpec((n_chain * 2 * TOP_K, TM), lambda i, j: (step(i, j), 0)),
                  pl.BlockSpec((n_chain, N_EXPERTS, 1), lambda i, j: (step(i, j), 0, 0))]

    return pl.pallas_call(
        functools.partial(_mixer_body, n_chain, n_sub, sub_len, has_hist, pos0),
        grid=grid,
        in_specs=in_specs,
        out_specs=out_specs,
        out_shape=out_shape,
        scratch_shapes=[pltpu.VMEM((n_chain * n_sub, POOL_MAX + sub_len, POOL_WIDTH), F32)],
        compiler_params=pltpu.CompilerParams(dimension_semantics=("arbitrary", "arbitrary"),
                                             vmem_limit_bytes=VMEM_LIMIT),
        name="mixer_hist" if has_hist else "mixer",
    )(*args)


def _slot_matrix(info, weights):
    rows = lax.broadcasted_iota(jnp.int32, (RLOC, TM), 0)
    out = jnp.zeros((RLOC, TM), F32)
    for k in range(TOP_K):
        p_k = info[k:k + 1, :].astype(jnp.int32)
        out = out + jnp.where(p_k == rows, weights[k], 0.0)
    return out.astype(BF16)


def _run_copy(src_ref, src_row, dst_ref, dst_row, n_rows, sem):
    return pltpu.make_async_copy(src_ref.at[pl.ds(src_row, n_rows), :],
                                 dst_ref.at[pl.ds(dst_row, n_rows), :], sem)


def _dispatch_body(n_tiles, do_fill, n8_s, off_s, base_s, tot_s, fill_lo_s, fill_n_s, *refs):
    if do_fill:
        h2_ref, info_ref, _xb_in, xb_ref, xs_ref, zero_ref, sems = refs
    else:
        h2_ref, info_ref, xb_ref, xs_ref, zero_ref, sems = refs
    i = pl.program_id(0)
    slot = i % 2

    def wait_tile(t, sl):
        tot = pl.multiple_of(tot_s[t], ROW_ALIGN)
        _run_copy(xs_ref.at[sl], 0, xb_ref, 0, tot, sems.at[sl]).wait()

    @pl.when(i >= 2)
    def _():
        wait_tile(i - 2, slot)

    info = info_ref[...]
    ones = [1.0] * TOP_K
    perm = _slot_matrix(info, ones)
    xs_ref[slot] = _dot(perm, h2_ref[...].reshape(TM, D_MODEL))
    for e in range(N_EXPERTS):
        n = pl.multiple_of(n8_s[i * N_EXPERTS + e], ROW_ALIGN)

        @pl.when(n > 0)
        def _(e=e, n=n):
            src = pl.multiple_of(off_s[i * N_EXPERTS + e], ROW_ALIGN)
            dst = pl.multiple_of(base_s[i * N_EXPERTS + e], ROW_ALIGN)
            _run_copy(xs_ref.at[slot], src, xb_ref, dst, n, sems.at[slot]).start()

    @pl.when(i == n_tiles - 1)
    def _():
        if n_tiles >= 2:
            wait_tile(i - 1, 1 - slot)
        wait_tile(i, slot)
        if do_fill:
            zero_ref[...] = jnp.zeros(zero_ref.shape, F32)
            for e in range(N_EXPERTS):
                n = pl.multiple_of(fill_n_s[e], ROW_ALIGN)

                @pl.when(n > 0)
                def _(e=e, n=n):
                    dst = pl.multiple_of(fill_lo_s[e], ROW_ALIGN)
                    cp = _run_copy(zero_ref, 0, xb_ref, dst, n, sems.at[2])
                    cp.start()
                    cp.wait()


def _dispatch(h2, info, tables, xb, *, n_sub, sub_len, n_rows, do_fill):
    n_seq, seq_len, _ = h2.shape
    tiles_per_seq = seq_len // sub_len
    n_tiles = n_seq // n_sub * tiles_per_seq
    if n_sub == 1:
        hmap = lambda i, *_: (i // tiles_per_seq, i % tiles_per_seq, 0)
    else:
        hmap = lambda i, *_: (i, 0, 0)
    in_specs = [pl.BlockSpec((n_sub, sub_len, D_MODEL), hmap),
                pl.BlockSpec((2 * TOP_K, TM), lambda i, *_: (i, 0))]
    args = [h2, info]
    aliases = {}
    if xb is not None:
        in_specs.append(pl.BlockSpec(memory_space=pl.ANY))
        args.append(xb)
        aliases = {len(tables) + 2: 0}
    assert (xb is not None) == do_fill
    return pl.pallas_call(
        functools.partial(_dispatch_body, n_tiles, do_fill),
        grid_spec=pltpu.PrefetchScalarGridSpec(
            num_scalar_prefetch=len(tables),
            grid=(n_tiles,),
            in_specs=in_specs,
            out_specs=pl.BlockSpec(memory_space=pl.ANY),
            scratch_shapes=[pltpu.VMEM((2, RLOC, D_MODEL), F32),
                            pltpu.VMEM((BM, D_MODEL), F32),
                            pltpu.SemaphoreType.DMA((3,))],
        ),
        out_shape=jax.ShapeDtypeStruct((n_rows, D_MODEL), F32),
        input_output_aliases=aliases,
        compiler_params=pltpu.CompilerParams(dimension_semantics=("arbitrary",),
                                             vmem_limit_bytes=VMEM_LIMIT),
        name="dispatch_fill" if do_fill else "dispatch",
    )(*tables, *args)


def _expert_rows(xv, wup_bf, bup, wdn_bf, bdn):
    z = _dot(xv.astype(BF16), wup_bf[...]) + bup
    zg = jnp.minimum(z[:, :D_FF], SWIGLU_LIMIT)
    zl = jnp.clip(z[:, D_FF:], -SWIGLU_LIMIT, SWIGLU_LIMIT)
    act = zg * _sigmoid(SWIGLU_ALPHA * zg) * (zl + 1.0)
    return _dot(act.astype(BF16), wdn_bf[...]) + bdn


def _expert_body(ends_s, xb_ref, wup_ref, bup_ref, wdn_ref, bdn_ref, yb_ref, x_buf, y_buf, wup_bf, wdn_bf,
                 x_sem, y_sem):
    e = pl.program_id(0)
    n_total = ends_s[N_EXPERTS - 1]
    lo = jnp.where(e == 0, 0, ends_s[jnp.maximum(e - 1, 0)])
    hi = ends_s[e]

    def x_copy(blk, slot):
        return _run_copy(xb_ref, pl.multiple_of(blk * BM, BM), x_buf.at[slot], 0, BM, x_sem.at[slot])

    def y_copy(blk, slot):
        return _run_copy(y_buf.at[slot], 0, yb_ref, pl.multiple_of(blk * BM, BM), BM, y_sem.at[slot])

    @pl.when(e == 0)
    def _():
        x_copy(0, 0).start()

    @pl.when(hi > lo)
    def _():
        wup_bf[...] = wup_ref[...].astype(BF16)
        wdn_bf[...] = wdn_ref[...].astype(BF16)
        bup = bup_ref[...]
        bdn = bdn_ref[...]

        def block(blk, carry):
            slot = blk % 2
            x_copy(blk, slot).wait()

            @pl.when(blk + 1 < n_total)
            def _():
                x_copy(blk + 1, 1 - slot).start()

            @pl.when(blk >= 2)
            def _():
                y_copy(blk - 2, slot).wait()

            sub = BM // EXPERT_CHAINS
            for c in range(EXPERT_CHAINS):
                rows = pl.ds(c * sub, sub)
                y_buf[slot, rows, :] = _expert_rows(x_buf[slot, rows, :], wup_bf, bup, wdn_bf, bdn)
            y_copy(blk, slot).start()
            return carry

        lax.fori_loop(lo, hi, block, 0)

    @pl.when(e == N_EXPERTS - 1)
    def _():
        y_copy(n_total - 2, n_total % 2).wait()
        y_copy(n_total - 1, (n_total - 1) % 2).wait()


def _experts(xb, ends, w_up, b_up, w_down, b_down):
    n_rows = xb.shape[0]
    wmap = lambda e, ends: (e, 0, 0)
    return pl.pallas_call(
        _expert_body,
        grid_spec=pltpu.PrefetchScalarGridSpec(
            num_scalar_prefetch=1,
            grid=(N_EXPERTS,),
            in_specs=[pl.BlockSpec(memory_space=pl.ANY),
                      pl.BlockSpec((None, D_MODEL, 2 * D_FF), wmap),
                      pl.BlockSpec((None, 1, 2 * D_FF), wmap),
                      pl.BlockSpec((None, D_FF, D_MODEL), wmap),
                      pl.BlockSpec((None, 1, D_MODEL), wmap)],
            out_specs=pl.BlockSpec(memory_space=pl.ANY),
            scratch_shapes=[pltpu.VMEM((2, BM, D_MODEL), F32),
                            pltpu.VMEM((2, BM, D_MODEL), F32),
                            pltpu.VMEM((D_MODEL, 2 * D_FF), BF16),
                            pltpu.VMEM((D_FF, D_MODEL), BF16),
                            pltpu.SemaphoreType.DMA((2,)),
                            pltpu.SemaphoreType.DMA((2,))],
        ),
        out_shape=jax.ShapeDtypeStruct((n_rows, D_MODEL), F32),
        compiler_params=pltpu.CompilerParams(dimension_semantics=("arbitrary",),
                                             vmem_limit_bytes=VMEM_LIMIT),
        name="experts",
    )(ends, xb, w_up, b_up, w_down, b_down)


def _combine_body(n_tiles, n8_s, off_s, base_s, tot_s, yb_ref, info_ref, x1_ref, mod_ref, gpost_ref, o_ref,
                  yloc_ref, sems):
    i = pl.program_id(0)
    slot = i % 2

    def start_tile(t, sl):
        for e in range(N_EXPERTS):
            n = pl.multiple_of(n8_s[t * N_EXPERTS + e], ROW_ALIGN)

            @pl.when(n > 0)
            def _(e=e, n=n):
                src = pl.multiple_of(base_s[t * N_EXPERTS + e], ROW_ALIGN)
                dst = pl.multiple_of(off_s[t * N_EXPERTS + e], ROW_ALIGN)
                _run_copy(yb_ref, src, yloc_ref.at[sl], dst, n, sems.at[sl]).start()

    @pl.when(i == 0)
    def _():
        yloc_ref[:, TM * TOP_K:RLOC, :] = jnp.zeros((2, RLOC - TM * TOP_K, D_MODEL), F32)
        start_tile(0, 0)

    @pl.when(i + 1 < n_tiles)
    def _():
        start_tile(i + 1, 1 - slot)

    tot = pl.multiple_of(tot_s[i], ROW_ALIGN)
    _run_copy(yb_ref, 0, yloc_ref.at[slot], 0, tot, sems.at[slot]).wait()

    info = info_ref[...]
    wmat = _slot_matrix(info, [info[TOP_K + k:TOP_K + k + 1, :] for k in range(TOP_K)])
    f = lax.dot_general(wmat, yloc_ref[slot].astype(BF16), (((0,), (0,)), ((), ())),
                        preferred_element_type=F32)
    x1 = x1_ref[...]
    gate_f = mod_ref[...][:, 5:6]
    o_ref[...] = x1 + gate_f * _rms(f.reshape(x1.shape), gpost_ref[...])


def _combine(yb, info, x1, mod, g_post, tables, *, n_sub, sub_len):
    n_seq, seq_len, _ = x1.shape
    tiles_per_seq = seq_len // sub_len
    n_tiles = n_seq // n_sub * tiles_per_seq
    if n_sub == 1:
        xmap = lambda i, *_: (i // tiles_per_seq, i % tiles_per_seq, 0)
        mmap = lambda i, *_: (i // tiles_per_seq, 0, 0)
    else:
        xmap = lambda i, *_: (i, 0, 0)
        mmap = lambda i, *_: (i, 0, 0)
    return pl.pallas_call(
        functools.partial(_combine_body, n_tiles),
        grid_spec=pltpu.PrefetchScalarGridSpec(
            num_scalar_prefetch=len(tables),
            grid=(n_tiles,),
            in_specs=[pl.BlockSpec(memory_space=pl.ANY),
                      pl.BlockSpec((2 * TOP_K, TM), lambda i, *_: (i, 0)),
                      pl.BlockSpec((n_sub, sub_len, D_MODEL), xmap),
                      pl.BlockSpec((n_sub, N_MOD, D_MODEL), mmap),
                      pl.BlockSpec((1, D_MODEL), lambda i, *_: (0, 0))],
            out_specs=pl.BlockSpec((n_sub, sub_len, D_MODEL), xmap),
            scratch_shapes=[pltpu.VMEM((2, RLOC, D_MODEL), F32),
                            pltpu.SemaphoreType.DMA((2,))],
        ),
        out_shape=jax.ShapeDtypeStruct(x1.shape, F32),
        compiler_params=pltpu.CompilerParams(dimension_semantics=("arbitrary",),
                                             vmem_limit_bytes=VMEM_LIMIT),
        name="combine",
    )(*tables, yb, info, x1, mod, g_post)


def _tables(n8_all):
    n_tiles = n8_all.shape[0]
    off = jnp.cumsum(n8_all, axis=1) - n8_all
    tot = jnp.sum(n8_all, axis=1)
    cnt = jnp.sum(n8_all, axis=0)
    region = (cnt + BM - 1) // BM * BM
    start = jnp.cumsum(region) - region
    base = start[None, :] + jnp.cumsum(n8_all, axis=0) - n8_all
    fill_lo = start + cnt
    fill_n = region - cnt
    ends = jnp.cumsum(region // BM)
    nblk = ends[-1]
    return dict(n8=n8_all, off=off, base=base, tot=tot, fill_lo=fill_lo, fill_n=fill_n, ends=ends, nblk=nblk)


def _max_rows(n_tokens, n_tiles):
    worst = n_tokens * TOP_K + n_tiles * N_EXPERTS * (ROW_ALIGN - 1) + N_EXPERTS * (BM - ROW_ALIGN)
    return (worst + BM - 1) // BM * BM


def kernel(x_prompt, x_sample, cache_pool, c_prompt, c_sample, w_ada, b_ada, g_pre_mix, g_post_mix, g_pre_ffn, g_post_ffn, w_in, w_pool, s_pool, g_v, b_v, w_s, b_s, p_a, p_b, w_o, w_router, b_router, w_up, b_up, w_down, b_down):
    assert w_ada.shape[0] == 1, "single-layer trunk"
    n_p, n_s = x_prompt.shape[0], x_sample.shape[0]
    past_len = x_prompt.shape[1]
    i32 = jnp.int32

    mod = _ada(jnp.concatenate([c_prompt, c_sample], axis=0), w_ada.reshape(D_MODEL, N_MOD * D_MODEL), b_ada)
    mod = mod.reshape(n_p + n_s, N_MOD, D_MODEL)
    mod_p, mod_s = mod[:n_p], mod[n_p:]

    row = lambda v: v.reshape(1, -1)
    weights = (row(g_pre_mix[0]), w_in[0].astype(BF16), w_pool[0].astype(BF16), row(s_pool[0]), row(g_v[0]),
               row(b_v[0]), w_s[0].astype(BF16), b_s[0].T, p_a[0].astype(BF16), p_b[0].astype(BF16),
               w_o[0].astype(BF16), row(g_post_mix[0]), row(g_pre_ffn[0]), w_router[0].T.astype(BF16),
               b_router[0].reshape(-1, 1))

    sub_s = x_sample.shape[1]
    x1_p, h2_p, pool_p, info_p, n8_p = _mixer(x_prompt, mod_p, None, weights, n_sub=1, sub_len=TM, pos0=0)
    x1_s, h2_s, pool_s, v_s, info_s, n8_s = _mixer(x_sample, mod_s, cache_pool[0], weights,
                                                   n_sub=TM // sub_s, sub_len=sub_s, pos0=past_len)

    tiles_p, tiles_s = n8_p.shape[0], n8_s.shape[0]
    n_tiles = tiles_p + tiles_s
    tb = _tables(jnp.concatenate([n8_p, n8_s], axis=0).reshape(n_tiles, N_EXPERTS))
    n_rows = _max_rows(n_tiles * TM, n_tiles)
    n_blocks = n_rows // BM

    def tile_tables(lo, hi):
        return (tb["n8"][lo:hi].reshape(-1).astype(i32), tb["off"][lo:hi].reshape(-1).astype(i32),
                tb["base"][lo:hi].reshape(-1).astype(i32), tb["tot"][lo:hi].astype(i32))

    fill = (tb["fill_lo"].astype(i32), tb["fill_n"].astype(i32))
    xb = _dispatch(h2_p, info_p, tile_tables(0, tiles_p) + fill, None,
                   n_sub=1, sub_len=TM, n_rows=n_rows, do_fill=False)
    xb = _dispatch(h2_s, info_s, tile_tables(tiles_p, n_tiles) + fill, xb,
                   n_sub=TM // sub_s, sub_len=sub_s, n_rows=n_rows, do_fill=True)

    yb = _experts(xb, tb["ends"].astype(i32), w_up.reshape(N_EXPERTS, D_MODEL, 2 * D_FF),
                  b_up.reshape(N_EXPERTS, 1, 2 * D_FF), w_down.reshape(N_EXPERTS, D_FF, D_MODEL),
                  b_down.reshape(N_EXPERTS, 1, D_MODEL))

    g_post = row(g_post_ffn[0])
    y_p = _combine(yb, info_p, x1_p, mod_p, g_post, tile_tables(0, tiles_p), n_sub=1, sub_len=TM)
    y_s = _combine(yb, info_s, x1_s, mod_s, g_post, tile_tables(tiles_p, n_tiles),
                   n_sub=TM // sub_s, sub_len=sub_s)

    return (y_p, y_s, pool_p[None], pool_s[None], v_s.reshape(1, n_s, sub_s, GMLP_WIDTH))
```

```python
import functools

import jax
import jax.numpy as jnp
from jax import lax
from jax.experimental import pallas as pl
from jax.experimental.pallas import tpu as pltpu

D_MODEL = 1024
POOL_WINDOWS = (2, 4, 8, 16)
POOL_MAX = 16
POOL_GDIM = 128
POOL_WIDTH = 512
GMLP_HEADS = 4
GMLP_CHUNK = 128
GMLP_WIDTH = 512
N_EXPERTS = 32
TOP_K = 4
D_FF = 1024
SWIGLU_LIMIT = 7.0
SWIGLU_ALPHA = 1.702
NORM_EPS = 1e-6
LN_EPS = 1e-5
N_MOD = 6

TM = 256
ROW_ALIGN = 8
RLOC = 1280
BM = 512
EXPERT_CHAINS = 1
MIXER_CHAINS = 4
VMEM_LIMIT = 56 * 1024 * 1024

BF16 = jnp.bfloat16
F32 = jnp.float32


def _dot(a, b):
    return jnp.dot(a, b, preferred_element_type=F32)


def _rms(x, g):
    ms = jnp.mean(x * x, axis=-1, keepdims=True)
    return x * lax.rsqrt(ms + NORM_EPS) * g


def _sigmoid(x):
    return 0.5 * jnp.tanh(0.5 * x) + 0.5


def _gelu(x):
    return 0.5 * x * (1.0 + lax.erf(x * 0.7071067811865476))


def _ada_body(c_ref, w_ref, b_ref, o_ref):
    c = c_ref[...]
    s = c * jax.nn.sigmoid(c)
    o_ref[...] = _dot(s.astype(BF16), w_ref[...].astype(BF16)) + b_ref[...]


def _ada(c_all, w_ada, b_ada):
    n = c_all.shape[0]
    nt = N_MOD * D_MODEL // D_MODEL
    return pl.pallas_call(
        _ada_body,
        grid=(nt,),
        in_specs=[pl.BlockSpec((n, D_MODEL), lambda i: (0, 0)),
                  pl.BlockSpec((D_MODEL, D_MODEL), lambda i: (0, i)),
                  pl.BlockSpec((1, D_MODEL), lambda i: (0, i))],
        out_specs=pl.BlockSpec((n, D_MODEL), lambda i: (0, i)),
        out_shape=jax.ShapeDtypeStruct((n, N_MOD * D_MODEL), F32),
        compiler_params=pltpu.CompilerParams(dimension_semantics=("arbitrary",)),
        name="ada",
    )(c_all, w_ada, b_ada)


def _mixer_body(n_chain, n_sub, sub_len, has_hist, pos0, *refs):
    tiles = [_mixer_tile(c, n_chain, n_sub, sub_len, has_hist, pos0, *refs) for c in range(n_chain)]
    live = [True] * n_chain
    t = 0
    while any(live):
        for c, tile in enumerate(tiles):
            if live[c] and t >= c:
                live[c] = next(tile, None) is not None
        t += 1


def _mixer_tile(c, n_chain, n_sub, sub_len, has_hist, pos0, *refs):
    it = iter(refs)
    x_ref = next(it)
    mod_ref = next(it)
    hist_ref = next(it) if has_hist else None
    (gpre_ref, win_ref, wpool_ref, spool_ref, gv_ref, bv_ref, ws_ref, bst_ref, pa_ref, pb_ref, wo_ref,
     gpost_ref, gffn_ref, wr_ref, br_ref) = [next(it) for _ in range(15)]
    x1_ref = next(it)
    h2_ref = next(it)
    pool_ref = next(it)
    v_ref = next(it) if has_hist else None
    info_ref = next(it)
    n8_ref = next(it)
    ext_ref = next(it)

    if has_hist:
        seqs, rows = slice(c * n_sub, (c + 1) * n_sub), slice(None)
        mod = mod_ref[seqs]
    else:
        seqs, rows = slice(None), slice(c * sub_len, (c + 1) * sub_len)
        mod = mod_ref[...]
    x = x_ref[seqs, rows, :]
    shift_m, scale_m, gate_m = mod[:, 0:1], mod[:, 1:2], mod[:, 2:3]
    shift_f, scale_f = mod[:, 3:4], mod[:, 4:5]

    h = _rms(x, gpre_ref[...]) * (1.0 + scale_m) + shift_m
    hb = h.reshape(TM, D_MODEL).astype(BF16)

    a = _dot(hb, win_ref[:, 0:POOL_WIDTH])
    yield True
    e0 = c * n_sub
    if has_hist:
        for s in range(n_sub):
            ext_ref[e0 + s, 0:1, :] = jnp.zeros((1, POOL_WIDTH), F32)
            ext_ref[e0 + s, 1:POOL_MAX, :] = hist_ref[e0 + s]
            ext_ref[e0 + s, POOL_MAX:POOL_MAX + sub_len, :] = a[s * sub_len:(s + 1) * sub_len]
            pool_ref[e0 + s] = a[(s + 1) * sub_len - (POOL_MAX - 1):(s + 1) * sub_len]
        row0 = pos0
    else:
        j = pl.program_id(1)
        prev = (c - 1) % n_chain
        if c == 0:
            @pl.when(j == 0)
            def _():
                ext_ref[0, 0:POOL_MAX, :] = jnp.zeros((POOL_MAX, POOL_WIDTH), F32)

            @pl.when(j > 0)
            def _():
                ext_ref[0, 0:POOL_MAX, :] = ext_ref[prev, sub_len:sub_len + POOL_MAX, :]
        else:
            ext_ref[c, 0:POOL_MAX, :] = ext_ref[prev, sub_len:sub_len + POOL_MAX, :]

        ext_ref[c, POOL_MAX:POOL_MAX + sub_len, :] = a
        if c == n_chain - 1:
            pool_ref[0] = a[sub_len - (POOL_MAX - 1):sub_len]
        row0 = pos0 + (j * n_chain + c) * sub_len

    pos = row0 + lax.broadcasted_iota(jnp.int32, (sub_len, 1), 0)
    spool = spool_ref[...]
    ya_rows = []
    for s in range(n_sub):
        ya_groups = []
        for g, w in enumerate(POOL_WINDOWS):
            lanes = slice(g * POOL_GDIM, (g + 1) * POOL_GDIM)
            win = ext_ref[e0 + s, POOL_MAX:POOL_MAX + sub_len, lanes]
            for back in range(1, w):
                win = win + ext_ref[e0 + s, POOL_MAX - back:POOL_MAX - back + sub_len, lanes]
            cnt = jnp.minimum(pos + 1, w).astype(F32)
            dlt = win / cnt - a[s * sub_len:(s + 1) * sub_len, lanes]
            ya_groups.append(_dot(dlt.astype(BF16), wpool_ref[g]) * spool[:, lanes])
        ya_rows.append(jnp.concatenate(ya_groups, axis=1))
    y_a = jnp.concatenate(ya_rows, axis=0) if n_sub > 1 else ya_rows[0]
    yield True

    u = _dot(hb, win_ref[:, POOL_WIDTH:POOL_WIDTH + GMLP_WIDTH])
    v = _dot(hb, win_ref[:, POOL_WIDTH + GMLP_WIDTH:POOL_WIDTH + 2 * GMLP_WIDTH])
    yield True
    u = _gelu(u)
    v = _gelu(v)
    mu = jnp.mean(v, axis=-1, keepdims=True)
    vc = v - mu
    var = jnp.mean(vc * vc, axis=-1, keepdims=True)
    vn = vc * lax.rsqrt(var + LN_EPS) * gv_ref[...] + bv_ref[...]
    if has_hist:
        v_ref[c * TM:(c + 1) * TM, :] = vn
    vb = vn.astype(BF16)
    seg = min(sub_len, GMLP_CHUNK)
    tri = (lax.broadcasted_iota(jnp.int32, (seg, seg), 0) >= lax.broadcasted_iota(jnp.int32, (seg, seg), 1))
    bst = bst_ref[...]
    yb_rows = []
    for q in range(TM // seg):
        srows = slice(q * seg, (q + 1) * seg)
        yb_heads = []
        for g in range(GMLP_HEADS):
            lanes = slice(g * 128, (g + 1) * 128)
            wm = jnp.where(tri, ws_ref[g, 0:seg, 0:seg], jnp.zeros((), BF16))
            sg = _dot(wm, vb[srows, lanes]) + bst[0:seg, g:g + 1]
            yb_heads.append(u[srows, lanes] * sg)
        yb_rows.append(jnp.concatenate(yb_heads, axis=1))
    y_b = jnp.concatenate(yb_rows, axis=0)
    yield True

    c0 = POOL_WIDTH + 2 * GMLP_WIDTH
    ga = _dot(hb, win_ref[:, c0:c0 + D_MODEL])
    gb = _dot(hb, win_ref[:, c0 + D_MODEL:c0 + 2 * D_MODEL])
    ma = _dot(y_a.astype(BF16), pa_ref[...])
    mb = _dot(y_b.astype(BF16), pb_ref[...])
    yield True
    m = _sigmoid(ga) * ma + _sigmoid(gb) * mb
    yield True
    mo = _dot(m.astype(BF16), wo_ref[...]).reshape(n_sub, sub_len, D_MODEL)
    yield True
    x1 = x + gate_m * _rms(mo, gpost_ref[...])
    x1_ref[seqs, rows, :] = x1

    h2 = (_rms(x1, gffn_ref[...]) * (1.0 + scale_f) + shift_f).astype(BF16)
    h2_ref[seqs, rows, :] = h2
    yield True
    logits = lax.dot_general(wr_ref[...], h2.reshape(TM, D_MODEL), (((1,), (1,)), ((), ())),
                             preferred_element_type=F32) + br_ref[...]

    erow = lax.broadcasted_iota(jnp.int32, (N_EXPERTS, TM), 0)
    work = logits
    onehots, vals = [], []
    for _ in range(TOP_K):
        mx = jnp.max(work, axis=0, keepdims=True)
        idx = jnp.min(jnp.where(work == mx, erow, N_EXPERTS), axis=0, keepdims=True)
        oh = erow == idx
        onehots.append(oh)
        vals.append(mx)
        work = jnp.where(oh, -jnp.inf, work)
    exps = [jnp.exp(vk - vals[0]) for vk in vals]
    den = exps[0] + exps[1] + exps[2] + exps[3]
    gates = [ek / den for ek in exps]

    sel = jnp.zeros((N_EXPERTS, TM), F32)
    for oh in onehots:
        sel = sel + oh.astype(F32)
    before = (lax.broadcasted_iota(jnp.int32, (TM, TM), 0) < lax.broadcasted_iota(jnp.int32, (TM, TM), 1))
    rank = _dot(sel.astype(BF16), before.astype(BF16))
    n_e = jnp.sum(sel, axis=1, keepdims=True).astype(jnp.int32)
    n8 = ((n_e + (ROW_ALIGN - 1)) // ROW_ALIGN) * ROW_ALIGN
    lower = (lax.broadcasted_iota(jnp.int32, (N_EXPERTS, N_EXPERTS), 0)
             > lax.broadcasted_iota(jnp.int32, (N_EXPERTS, N_EXPERTS), 1))
    n8b = jnp.broadcast_to(n8.astype(F32), (N_EXPERTS, 128)).astype(BF16)
    off = _dot(lower.astype(BF16), n8b)[:, 0:1]
    slot = off + rank
    irow = lax.broadcasted_iota(jnp.int32, (2 * TOP_K, TM), 0)
    info = jnp.zeros((2 * TOP_K, TM), F32)
    for k in range(TOP_K):
        p_k = jnp.sum(jnp.where(onehots[k], slot, 0.0), axis=0, keepdims=True)
        info = info + jnp.where(irow == k, p_k, 0.0) + jnp.where(irow == TOP_K + k, gates[k], 0.0)
    info_ref[c * 2 * TOP_K:(c + 1) * 2 * TOP_K, :] = info
    n8_ref[c] = n8


def _mixer(x, mod, hist, weights, *, n_sub, sub_len, pos0):
    n_seq, seq_len, _ = x.shape
    has_hist = hist is not None
    n_chain = MIXER_CHAINS
    assert n_sub * sub_len == TM
    if has_hist:
        assert seq_len == sub_len and n_seq % (n_sub * n_chain) == 0
        blk_seq, blk_rows = n_sub * n_chain, sub_len
        grid = (n_seq // blk_seq, 1)
        xmap = lambda i, j: (i, 0, 0)
        step = lambda i, j: i
    else:
        assert n_sub == 1 and seq_len % (sub_len * n_chain) == 0
        blk_seq, blk_rows = 1, sub_len * n_chain
        steps_per_seq = seq_len // blk_rows
        grid = (n_seq, steps_per_seq)
        xmap = lambda i, j: (i, j, 0)
        step = lambda i, j: i * steps_per_seq + j
    n_tiles = grid[0] * grid[1] * n_chain

    def full(arr):
        nd = arr.ndim
        return pl.BlockSpec(arr.shape, lambda i, j, _nd=nd: (0,) * _nd, pipeline_mode=pl.Buffered(1))

    in_specs = [pl.BlockSpec((blk_seq, blk_rows, D_MODEL), xmap),
                pl.BlockSpec((blk_seq, N_MOD, D_MODEL), lambda i, j: (i, 0, 0))]
    args = [x, mod]
    if has_hist:
        in_specs.append(pl.BlockSpec((blk_seq, POOL_MAX - 1, POOL_WIDTH), lambda i, j: (i, 0, 0)))
        args.append(hist)
    in_specs += [full(w) for w in weights]
    args += list(weights)

    out_shape = [jax.ShapeDtypeStruct(x.shape, F32),
                 jax.ShapeDtypeStruct(x.shape, BF16),
                 jax.ShapeDtypeStruct((n_seq, POOL_MAX - 1, POOL_WIDTH), F32)]
    out_specs = [pl.BlockSpec((blk_seq, blk_rows, D_MODEL), xmap),
                 pl.BlockSpec((blk_seq, blk_rows, D_MODEL), xmap),
                 pl.BlockSpec((blk_seq, POOL_MAX - 1, POOL_WIDTH), lambda i, j: (i, 0, 0))]
    if has_hist:
        out_shape.append(jax.ShapeDtypeStruct((n_seq * seq_len, GMLP_WIDTH), F32))
        out_specs.append(pl.BlockSpec((n_chain * TM, GMLP_WIDTH), lambda i, j: (i, 0)))
    out_shape += [jax.ShapeDtypeStruct((n_tiles * 2 * TOP_K, TM), F32),
                  jax.ShapeDtypeStruct((n_tiles, N_EXPERTS, 1), jnp.int32)]
    out_specs += [pl.BlockSpec((n_chain * 2 * TOP_K, TM), lambda i, j: (step(i, j), 0)),
                  pl.BlockSpec((n_chain, N_EXPERTS, 1), lambda i, j: (step(i, j), 0, 0))]

    return pl.pallas_call(
        functools.partial(_mixer_body, n_chain, n_sub, sub_len, has_hist, pos0),
        grid=grid,
        in_specs=in_specs,
        out_specs=out_specs,
        out_shape=out_shape,
        scratch_shapes=[pltpu.VMEM((n_chain * n_sub, POOL_MAX + sub_len, POOL_WIDTH), F32)],
        compiler_params=pltpu.CompilerParams(dimension_semantics=("arbitrary", "arbitrary"),
                                             vmem_limit_bytes=VMEM_LIMIT),
        name="mixer_hist" if has_hist else "mixer",
    )(*args)


def _slot_matrix(info, weights):
    rows = lax.broadcasted_iota(jnp.int32, (RLOC, TM), 0)
    out = jnp.zeros((RLOC, TM), F32)
    for k in range(TOP_K):
        p_k = info[k:k + 1, :].astype(jnp.int32)
        out = out + jnp.where(p_k == rows, weights[k], 0.0)
    return out.astype(BF16)


def _run_copy(src_ref, src_row, dst_ref, dst_row, n_rows, sem):
    return pltpu.make_async_copy(src_ref.at[pl.ds(src_row, n_rows), :],
                                 dst_ref.at[pl.ds(dst_row, n_rows), :], sem)


def _dispatch_body(n_tiles, aliased, n8_s, off_s, base_s, tot_s, *refs):
    if aliased:
        h2_ref, info_ref, _xb_in, xb_ref, xs_ref, sems = refs
    else:
        h2_ref, info_ref, xb_ref, xs_ref, sems = refs
    i = pl.program_id(0)
    slot = i % 2

    def wait_tile(t, sl):
        tot = pl.multiple_of(tot_s[t], ROW_ALIGN)
        _run_copy(xs_ref.at[sl], 0, xb_ref, 0, tot, sems.at[sl]).wait()

    @pl.when(i >= 2)
    def _():
        wait_tile(i - 2, slot)

    info = info_ref[...]
    ones = [1.0] * TOP_K
    perm = _slot_matrix(info, ones)
    xs_ref[slot] = _dot(perm, h2_ref[...].reshape(TM, D_MODEL))
    for e in range(N_EXPERTS):
        n = pl.multiple_of(n8_s[i * N_EXPERTS + e], ROW_ALIGN)

        @pl.when(n > 0)
        def _(e=e, n=n):
            src = pl.multiple_of(off_s[i * N_EXPERTS + e], ROW_ALIGN)
            dst = pl.multiple_of(base_s[i * N_EXPERTS + e], ROW_ALIGN)
            _run_copy(xs_ref.at[slot], src, xb_ref, dst, n, sems.at[slot]).start()

    @pl.when(i == n_tiles - 1)
    def _():
        if n_tiles >= 2:
            wait_tile(i - 1, 1 - slot)
        wait_tile(i, slot)


def _dispatch(h2, info, tables, xb, *, n_sub, sub_len, n_rows):
    n_seq, seq_len, _ = h2.shape
    tiles_per_seq = seq_len // sub_len
    n_tiles = n_seq // n_sub * tiles_per_seq
    if n_sub == 1:
        hmap = lambda i, *_: (i // tiles_per_seq, i % tiles_per_seq, 0)
    else:
        hmap = lambda i, *_: (i, 0, 0)
    in_specs = [pl.BlockSpec((n_sub, sub_len, D_MODEL), hmap),
                pl.BlockSpec((2 * TOP_K, TM), lambda i, *_: (i, 0))]
    args = [h2, info]
    aliases = {}
    if xb is not None:
        in_specs.append(pl.BlockSpec(memory_space=pl.ANY))
        args.append(xb)
        aliases = {len(tables) + 2: 0}
    return pl.pallas_call(
        functools.partial(_dispatch_body, n_tiles, xb is not None),
        grid_spec=pltpu.PrefetchScalarGridSpec(
            num_scalar_prefetch=len(tables),
            grid=(n_tiles,),
            in_specs=in_specs,
            out_specs=pl.BlockSpec(memory_space=pl.ANY),
            scratch_shapes=[pltpu.VMEM((2, RLOC, D_MODEL), F32),
                            pltpu.SemaphoreType.DMA((2,))],
        ),
        out_shape=jax.ShapeDtypeStruct((n_rows, D_MODEL), F32),
        input_output_aliases=aliases,
        compiler_params=pltpu.CompilerParams(dimension_semantics=("arbitrary",),
                                             vmem_limit_bytes=VMEM_LIMIT),
        name="dispatch_more" if xb is not None else "dispatch",
    )(*tables, *args)


def _expert_rows(xv, wup_bf, bup, wdn_bf, bdn):
    z = _dot(xv.astype(BF16), wup_bf[...]) + bup
    zg = jnp.minimum(z[:, :D_FF], SWIGLU_LIMIT)
    zl = jnp.clip(z[:, D_FF:], -SWIGLU_LIMIT, SWIGLU_LIMIT)
    act = zg * _sigmoid(SWIGLU_ALPHA * zg) * (zl + 1.0)
    return _dot(act.astype(BF16), wdn_bf[...]) + bdn


def _expert_body(ends_s, rows_s, xb_ref, wup_ref, bup_ref, wdn_ref, bdn_ref, yb_ref, x_buf, y_buf, wup_bf,
                 wdn_bf, x_sem, y_sem):
    e = pl.program_id(0)
    n_total = ends_s[N_EXPERTS - 1]
    lo = jnp.where(e == 0, 0, ends_s[jnp.maximum(e - 1, 0)])
    hi = ends_s[e]

    def x_copy(blk, slot):
        n = pl.multiple_of(rows_s[blk], ROW_ALIGN)
        return _run_copy(xb_ref, pl.multiple_of(blk * BM, BM), x_buf.at[slot], 0, n, x_sem.at[slot])

    def y_copy(blk, slot):
        n = pl.multiple_of(rows_s[blk], ROW_ALIGN)
        return _run_copy(y_buf.at[slot], 0, yb_ref, pl.multiple_of(blk * BM, BM), n, y_sem.at[slot])

    @pl.when(e == 0)
    def _():
        x_buf[...] = jnp.zeros(x_buf.shape, F32)
        x_copy(0, 0).start()

    @pl.when(hi > lo)
    def _():
        wup_bf[...] = wup_ref[...].astype(BF16)
        wdn_bf[...] = wdn_ref[...].astype(BF16)
        bup = bup_ref[...]
        bdn = bdn_ref[...]

        def block(blk, carry):
            slot = blk % 2
            x_copy(blk, slot).wait()

            @pl.when(blk + 1 < n_total)
            def _():
                x_copy(blk + 1, 1 - slot).start()

            @pl.when(blk >= 2)
            def _():
                y_copy(blk - 2, slot).wait()

            sub = BM // EXPERT_CHAINS
            for c in range(EXPERT_CHAINS):
                rows = pl.ds(c * sub, sub)
                y_buf[slot, rows, :] = _expert_rows(x_buf[slot, rows, :], wup_bf, bup, wdn_bf, bdn)
            y_copy(blk, slot).start()
            return carry

        lax.fori_loop(lo, hi, block, 0)

    @pl.when(e == N_EXPERTS - 1)
    def _():
        y_copy(n_total - 2, n_total % 2).wait()
        y_copy(n_total - 1, (n_total - 1) % 2).wait()


def _experts(xb, ends, blk_rows, w_up, b_up, w_down, b_down):
    n_rows = xb.shape[0]
    wmap = lambda e, *_: (e, 0, 0)
    return pl.pallas_call(
        _expert_body,
        grid_spec=pltpu.PrefetchScalarGridSpec(
            num_scalar_prefetch=2,
            grid=(N_EXPERTS,),
            in_specs=[pl.BlockSpec(memory_space=pl.ANY),
                      pl.BlockSpec((None, D_MODEL, 2 * D_FF), wmap),
                      pl.BlockSpec((None, 1, 2 * D_FF), wmap),
                      pl.BlockSpec((None, D_FF, D_MODEL), wmap),
                      pl.BlockSpec((None, 1, D_MODEL), wmap)],
            out_specs=pl.BlockSpec(memory_space=pl.ANY),
            scratch_shapes=[pltpu.VMEM((2, BM, D_MODEL), F32),
                            pltpu.VMEM((2, BM, D_MODEL), F32),
                            pltpu.VMEM((D_MODEL, 2 * D_FF), BF16),
                            pltpu.VMEM((D_FF, D_MODEL), BF16),
                            pltpu.SemaphoreType.DMA((2,)),
                            pltpu.SemaphoreType.DMA((2,))],
        ),
        out_shape=jax.ShapeDtypeStruct((n_rows, D_MODEL), F32),
        compiler_params=pltpu.CompilerParams(dimension_semantics=("arbitrary",),
                                             vmem_limit_bytes=VMEM_LIMIT),
        name="experts",
    )(ends, blk_rows, xb, w_up, b_up, w_down, b_down)


def _combine_body(n_tiles, n8_s, off_s, base_s, tot_s, yb_ref, info_ref, x1_ref, mod_ref, gpost_ref, o_ref,
                  yloc_ref, sems):
    i = pl.program_id(0)
    slot = i % 2

    def start_tile(t, sl):
        for e in range(N_EXPERTS):
            n = pl.multiple_of(n8_s[t * N_EXPERTS + e], ROW_ALIGN)

            @pl.when(n > 0)
            def _(e=e, n=n):
                src = pl.multiple_of(base_s[t * N_EXPERTS + e], ROW_ALIGN)
                dst = pl.multiple_of(off_s[t * N_EXPERTS + e], ROW_ALIGN)
                _run_copy(yb_ref, src, yloc_ref.at[sl], dst, n, sems.at[sl]).start()

    @pl.when(i == 0)
    def _():
        yloc_ref[:, TM * TOP_K:RLOC, :] = jnp.zeros((2, RLOC - TM * TOP_K, D_MODEL), F32)
        start_tile(0, 0)

    @pl.when(i + 1 < n_tiles)
    def _():
        start_tile(i + 1, 1 - slot)

    tot = pl.multiple_of(tot_s[i], ROW_ALIGN)
    _run_copy(yb_ref, 0, yloc_ref.at[slot], 0, tot, sems.at[slot]).wait()

    info = info_ref[...]
    wmat = _slot_matrix(info, [info[TOP_K + k:TOP_K + k + 1, :] for k in range(TOP_K)])
    f = lax.dot_general(wmat, yloc_ref[slot].astype(BF16), (((0,), (0,)), ((), ())),
                        preferred_element_type=F32)
    x1 = x1_ref[...]
    gate_f = mod_ref[...][:, 5:6]
    o_ref[...] = x1 + gate_f * _rms(f.reshape(x1.shape), gpost_ref[...])


def _combine(yb, info, x1, mod, g_post, tables, *, n_sub, sub_len):
    n_seq, seq_len, _ = x1.shape
    tiles_per_seq = seq_len // sub_len
    n_tiles = n_seq // n_sub * tiles_per_seq
    if n_sub == 1:
        xmap = lambda i, *_: (i // tiles_per_seq, i % tiles_per_seq, 0)
        mmap = lambda i, *_: (i // tiles_per_seq, 0, 0)
    else:
        xmap = lambda i, *_: (i, 0, 0)
        mmap = lambda i, *_: (i, 0, 0)
    return pl.pallas_call(
        functools.partial(_combine_body, n_tiles),
        grid_spec=pltpu.PrefetchScalarGridSpec(
            num_scalar_prefetch=len(tables),
            grid=(n_tiles,),
            in_specs=[pl.BlockSpec(memory_space=pl.ANY),
                      pl.BlockSpec((2 * TOP_K, TM), lambda i, *_: (i, 0)),
                      pl.BlockSpec((n_sub, sub_len, D_MODEL), xmap),
                      pl.BlockSpec((n_sub, N_MOD, D_MODEL), mmap),
                      pl.BlockSpec((1, D_MODEL), lambda i, *_: (0, 0))],
            out_specs=pl.BlockSpec((n_sub, sub_len, D_MODEL), xmap),
            scratch_shapes=[pltpu.VMEM((2, RLOC, D_MODEL), F32),
                            pltpu.SemaphoreType.DMA((2,))],
        ),
        out_shape=jax.ShapeDtypeStruct(x1.shape, F32),
        compiler_params=pltpu.CompilerParams(dimension_semantics=("arbitrary",),
                                             vmem_limit_bytes=VMEM_LIMIT),
        name="combine",
    )(*tables, yb, info, x1, mod, g_post)


def _tables(n8_all):
    n_tiles = n8_all.shape[0]
    off = jnp.cumsum(n8_all, axis=1) - n8_all
    tot = jnp.sum(n8_all, axis=1)
    cnt = jnp.sum(n8_all, axis=0)
    region = (cnt + BM - 1) // BM * BM
    start = jnp.cumsum(region) - region
    base = start[None, :] + jnp.cumsum(n8_all, axis=0) - n8_all
    ends = jnp.cumsum(region // BM)
    return dict(n8=n8_all, off=off, base=base, tot=tot, cnt=cnt, region=region, ends=ends)


def _max_rows(n_tokens, n_tiles):
    worst = n_tokens * TOP_K + n_tiles * N_EXPERTS * (ROW_ALIGN - 1) + N_EXPERTS * (BM - ROW_ALIGN)
    return (worst + BM - 1) // BM * BM


def kernel(x_prompt, x_sample, cache_pool, c_prompt, c_sample, w_ada, b_ada, g_pre_mix, g_post_mix, g_pre_ffn, g_post_ffn, w_in, w_pool, s_pool, g_v, b_v, w_s, b_s, p_a, p_b, w_o, w_router, b_router, w_up, b_up, w_down, b_down):
    assert w_ada.shape[0] == 1, "single-layer trunk"
    n_p, n_s = x_prompt.shape[0], x_sample.shape[0]
    past_len = x_prompt.shape[1]
    i32 = jnp.int32

    mod = _ada(jnp.concatenate([c_prompt, c_sample], axis=0), w_ada.reshape(D_MODEL, N_MOD * D_MODEL), b_ada)
    mod = mod.reshape(n_p + n_s, N_MOD, D_MODEL)
    mod_p, mod_s = mod[:n_p], mod[n_p:]

    row = lambda v: v.reshape(1, -1)
    weights = (row(g_pre_mix[0]), w_in[0].astype(BF16), w_pool[0].astype(BF16), row(s_pool[0]), row(g_v[0]),
               row(b_v[0]), w_s[0].astype(BF16), b_s[0].T, p_a[0].astype(BF16), p_b[0].astype(BF16),
               w_o[0].astype(BF16), row(g_post_mix[0]), row(g_pre_ffn[0]), w_router[0].T.astype(BF16),
               b_router[0].reshape(-1, 1))

    sub_s = x_sample.shape[1]
    x1_p, h2_p, pool_p, info_p, n8_p = _mixer(x_prompt, mod_p, None, weights, n_sub=1, sub_len=TM, pos0=0)
    x1_s, h2_s, pool_s, v_s, info_s, n8_s = _mixer(x_sample, mod_s, cache_pool[0], weights,
                                                   n_sub=TM // sub_s, sub_len=sub_s, pos0=past_len)

    tiles_p, tiles_s = n8_p.shape[0], n8_s.shape[0]
    n_tiles = tiles_p + tiles_s
    tb = _tables(jnp.concatenate([n8_p, n8_s], axis=0).reshape(n_tiles, N_EXPERTS))
    n_rows = _max_rows(n_tiles * TM, n_tiles)
    n_blocks = n_rows // BM

    def tile_tables(lo, hi):
        return (tb["n8"][lo:hi].reshape(-1).astype(i32), tb["off"][lo:hi].reshape(-1).astype(i32),
                tb["base"][lo:hi].reshape(-1).astype(i32), tb["tot"][lo:hi].astype(i32))

    xb = _dispatch(h2_p, info_p, tile_tables(0, tiles_p), None, n_sub=1, sub_len=TM, n_rows=n_rows)
    xb = _dispatch(h2_s, info_s, tile_tables(tiles_p, n_tiles), xb,
                   n_sub=TM // sub_s, sub_len=sub_s, n_rows=n_rows)

    blk = jnp.arange(n_blocks, dtype=i32)
    first = tb["ends"] - tb["region"] // BM
    mine = ((first[None, :] <= blk[:, None]) & (blk[:, None] < tb["ends"][None, :])).astype(i32)
    left = jnp.sum(mine * (tb["cnt"][None, :] - (blk[:, None] - first[None, :]) * BM), axis=1)
    blk_rows = jnp.clip(left, 0, BM).astype(i32)
    yb = _experts(xb, tb["ends"].astype(i32), blk_rows, w_up.reshape(N_EXPERTS, D_MODEL, 2 * D_FF),
                  b_up.reshape(N_EXPERTS, 1, 2 * D_FF), w_down.reshape(N_EXPERTS, D_FF, D_MODEL),
                  b_down.reshape(N_EXPERTS, 1, D_MODEL))

    g_post = row(g_post_ffn[0])
    y_p = _combine(yb, info_p, x1_p, mod_p, g_post, tile_tables(0, tiles_p), n_sub=1, sub_len=TM)
    y_s = _combine(yb, info_s, x1_s, mod_s, g_post, tile_tables(tiles_p, n_tiles),
                   n_sub=TM // sub_s, sub_len=sub_s)

    return (y_p, y_s, pool_p[None], pool_s[None], v_s.reshape(1, n_s, sub_s, GMLP_WIDTH))
```

```python
import functools

import jax
import jax.numpy as jnp
from jax import lax
from jax.experimental import pallas as pl
from jax.experimental.pallas import tpu as pltpu

D_MODEL = 1024
POOL_WINDOWS = (2, 4, 8, 16)
POOL_MAX = 16
POOL_GDIM = 128
POOL_WIDTH = 512
GMLP_HEADS = 4
GMLP_CHUNK = 128
GMLP_WIDTH = 512
N_EXPERTS = 32
TOP_K = 4
D_FF = 1024
SWIGLU_LIMIT = 7.0
SWIGLU_ALPHA = 1.702
NORM_EPS = 1e-6
LN_EPS = 1e-5
N_MOD = 6

TM = 256
ROW_ALIGN = 8
RLOC = 1280
BM = 512
X_SLOTS = 3
MIXER_CHAINS = 4
VMEM_LIMIT = 56 * 1024 * 1024

BF16 = jnp.bfloat16
F32 = jnp.float32


def _dot(a, b):
    return jnp.dot(a, b, preferred_element_type=F32)


def _rms(x, g):
    ms = jnp.mean(x * x, axis=-1, keepdims=True)
    return x * lax.rsqrt(ms + NORM_EPS) * g


def _sigmoid(x):
    return 0.5 * jnp.tanh(0.5 * x) + 0.5


def _gelu(x):
    return 0.5 * x * (1.0 + lax.erf(x * 0.7071067811865476))


def _ada_body(c_ref, w_ref, b_ref, o_ref):
    c = c_ref[...]
    s = c * jax.nn.sigmoid(c)
    o_ref[...] = _dot(s.astype(BF16), w_ref[...].astype(BF16)) + b_ref[...]


def _ada(c_all, w_ada, b_ada):
    n = c_all.shape[0]
    nt = N_MOD * D_MODEL // D_MODEL
    return pl.pallas_call(
        _ada_body,
        grid=(nt,),
        in_specs=[pl.BlockSpec((n, D_MODEL), lambda i: (0, 0)),
                  pl.BlockSpec((D_MODEL, D_MODEL), lambda i: (0, i)),
                  pl.BlockSpec((1, D_MODEL), lambda i: (0, i))],
        out_specs=pl.BlockSpec((n, D_MODEL), lambda i: (0, i)),
        out_shape=jax.ShapeDtypeStruct((n, N_MOD * D_MODEL), F32),
        compiler_params=pltpu.CompilerParams(dimension_semantics=("arbitrary",)),
        name="ada",
    )(c_all, w_ada, b_ada)


def _mixer_body(n_chain, n_sub, sub_len, has_hist, pos0, *refs):
    tiles = [_mixer_tile(c, n_chain, n_sub, sub_len, has_hist, pos0, *refs) for c in range(n_chain)]
    live = [True] * n_chain
    t = 0
    while any(live):
        for c, tile in enumerate(tiles):
            if live[c] and t >= c:
                live[c] = next(tile, None) is not None
        t += 1


def _mixer_tile(c, n_chain, n_sub, sub_len, has_hist, pos0, *refs):
    it = iter(refs)
    x_ref = next(it)
    mod_ref = next(it)
    hist_ref = next(it) if has_hist else None
    (gpre_ref, win_ref, wpool_ref, spool_ref, gv_ref, bv_ref, ws_ref, bst_ref, pa_ref, pb_ref, wo_ref,
     gpost_ref, gffn_ref, wr_ref, br_ref) = [next(it) for _ in range(15)]
    x1_ref = next(it)
    h2_ref = next(it)
    pool_ref = next(it)
    v_ref = next(it) if has_hist else None
    info_ref = next(it)
    n8_ref = next(it)
    ext_ref = next(it)

    if has_hist:
        seqs, rows = slice(c * n_sub, (c + 1) * n_sub), slice(None)
        mod = mod_ref[seqs]
    else:
        seqs, rows = slice(None), slice(c * sub_len, (c + 1) * sub_len)
        mod = mod_ref[...]
    x = x_ref[seqs, rows, :]
    shift_m, scale_m, gate_m = mod[:, 0:1], mod[:, 1:2], mod[:, 2:3]
    shift_f, scale_f = mod[:, 3:4], mod[:, 4:5]

    h = _rms(x, gpre_ref[...]) * (1.0 + scale_m) + shift_m
    hb = h.reshape(TM, D_MODEL).astype(BF16)

    a = _dot(hb, win_ref[:, 0:POOL_WIDTH])
    yield True
    e0 = c * n_sub
    if has_hist:
        for s in range(n_sub):
            ext_ref[e0 + s, 0:1, :] = jnp.zeros((1, POOL_WIDTH), F32)
            ext_ref[e0 + s, 1:POOL_MAX, :] = hist_ref[e0 + s]
            ext_ref[e0 + s, POOL_MAX:POOL_MAX + sub_len, :] = a[s * sub_len:(s + 1) * sub_len]
            pool_ref[e0 + s] = a[(s + 1) * sub_len - (POOL_MAX - 1):(s + 1) * sub_len]
        row0 = pos0
    else:
        j = pl.program_id(1)
        prev = (c - 1) % n_chain
        if c == 0:
            @pl.when(j == 0)
            def _():
                ext_ref[0, 0:POOL_MAX, :] = jnp.zeros((POOL_MAX, POOL_WIDTH), F32)

            @pl.when(j > 0)
            def _():
                ext_ref[0, 0:POOL_MAX, :] = ext_ref[prev, sub_len:sub_len + POOL_MAX, :]
        else:
            ext_ref[c, 0:POOL_MAX, :] = ext_ref[prev, sub_len:sub_len + POOL_MAX, :]

        ext_ref[c, POOL_MAX:POOL_MAX + sub_len, :] = a
        if c == n_chain - 1:
            pool_ref[0] = a[sub_len - (POOL_MAX - 1):sub_len]
        row0 = pos0 + (j * n_chain + c) * sub_len

    pos = row0 + lax.broadcasted_iota(jnp.int32, (sub_len, 1), 0)
    spool = spool_ref[...]
    ya_rows = []
    for s in range(n_sub):
        ya_groups = []
        for g, w in enumerate(POOL_WINDOWS):
            lanes = slice(g * POOL_GDIM, (g + 1) * POOL_GDIM)
            win = ext_ref[e0 + s, POOL_MAX:POOL_MAX + sub_len, lanes]
            for back in range(1, w):
                win = win + ext_ref[e0 + s, POOL_MAX - back:POOL_MAX - back + sub_len, lanes]
            cnt = jnp.minimum(pos + 1, w).astype(F32)
            dlt = win / cnt - a[s * sub_len:(s + 1) * sub_len, lanes]
            ya_groups.append(_dot(dlt.astype(BF16), wpool_ref[g]) * spool[:, lanes])
        ya_rows.append(jnp.concatenate(ya_groups, axis=1))
    y_a = jnp.concatenate(ya_rows, axis=0) if n_sub > 1 else ya_rows[0]
    yield True

    u = _dot(hb, win_ref[:, POOL_WIDTH:POOL_WIDTH + GMLP_WIDTH])
    v = _dot(hb, win_ref[:, POOL_WIDTH + GMLP_WIDTH:POOL_WIDTH + 2 * GMLP_WIDTH])
    yield True
    u = _gelu(u)
    v = _gelu(v)
    mu = jnp.mean(v, axis=-1, keepdims=True)
    vc = v - mu
    var = jnp.mean(vc * vc, axis=-1, keepdims=True)
    vn = vc * lax.rsqrt(var + LN_EPS) * gv_ref[...] + bv_ref[...]
    if has_hist:
        v_ref[c * TM:(c + 1) * TM, :] = vn
    vb = vn.astype(BF16)
    seg = min(sub_len, GMLP_CHUNK)
    tri = (lax.broadcasted_iota(jnp.int32, (seg, seg), 0) >= lax.broadcasted_iota(jnp.int32, (seg, seg), 1))
    bst = bst_ref[...]
    yb_rows = []
    for q in range(TM // seg):
        srows = slice(q * seg, (q + 1) * seg)
        yb_heads = []
        for g in range(GMLP_HEADS):
            lanes = slice(g * 128, (g + 1) * 128)
            wm = jnp.where(tri, ws_ref[g, 0:seg, 0:seg], jnp.zeros((), BF16))
            sg = _dot(wm, vb[srows, lanes]) + bst[0:seg, g:g + 1]
            yb_heads.append(u[srows, lanes] * sg)
        yb_rows.append(jnp.concatenate(yb_heads, axis=1))
    y_b = jnp.concatenate(yb_rows, axis=0)
    yield True

    c0 = POOL_WIDTH + 2 * GMLP_WIDTH
    ga = _dot(hb, win_ref[:, c0:c0 + D_MODEL])
    gb = _dot(hb, win_ref[:, c0 + D_MODEL:c0 + 2 * D_MODEL])
    ma = _dot(y_a.astype(BF16), pa_ref[...])
    mb = _dot(y_b.astype(BF16), pb_ref[...])
    yield True
    m = _sigmoid(ga) * ma + _sigmoid(gb) * mb
    yield True
    mo = _dot(m.astype(BF16), wo_ref[...]).reshape(n_sub, sub_len, D_MODEL)
    yield True
    x1 = x + gate_m * _rms(mo, gpost_ref[...])
    x1_ref[seqs, rows, :] = x1

    h2 = (_rms(x1, gffn_ref[...]) * (1.0 + scale_f) + shift_f).astype(BF16)
    h2_ref[seqs, rows, :] = h2
    yield True
    logits = lax.dot_general(wr_ref[...], h2.reshape(TM, D_MODEL), (((1,), (1,)), ((), ())),
                             preferred_element_type=F32) + br_ref[...]

    erow = lax.broadcasted_iota(jnp.int32, (N_EXPERTS, TM), 0)
    work = logits
    onehots, vals = [], []
    for _ in range(TOP_K):
        mx = jnp.max(work, axis=0, keepdims=True)
        idx = jnp.min(jnp.where(work == mx, erow, N_EXPERTS), axis=0, keepdims=True)
        oh = erow == idx
        onehots.append(oh)
        vals.append(mx)
        work = jnp.where(oh, -jnp.inf, work)
    exps = [jnp.exp(vk - vals[0]) for vk in vals]
    den = exps[0] + exps[1] + exps[2] + exps[3]
    gates = [ek / den for ek in exps]

    sel = jnp.zeros((N_EXPERTS, TM), F32)
    for oh in onehots:
        sel = sel + oh.astype(F32)
    before = (lax.broadcasted_iota(jnp.int32, (TM, TM), 0) < lax.broadcasted_iota(jnp.int32, (TM, TM), 1))
    rank = _dot(sel.astype(BF16), before.astype(BF16))
    n_e = jnp.sum(sel, axis=1, keepdims=True).astype(jnp.int32)
    n8 = ((n_e + (ROW_ALIGN - 1)) // ROW_ALIGN) * ROW_ALIGN
    lower = (lax.broadcasted_iota(jnp.int32, (N_EXPERTS, N_EXPERTS), 0)
             > lax.broadcasted_iota(jnp.int32, (N_EXPERTS, N_EXPERTS), 1))
    n8b = jnp.broadcast_to(n8.astype(F32), (N_EXPERTS, 128)).astype(BF16)
    off = _dot(lower.astype(BF16), n8b)[:, 0:1]
    slot = off + rank
    irow = lax.broadcasted_iota(jnp.int32, (2 * TOP_K, TM), 0)
    info = jnp.zeros((2 * TOP_K, TM), F32)
    for k in range(TOP_K):
        p_k = jnp.sum(jnp.where(onehots[k], slot, 0.0), axis=0, keepdims=True)
        info = info + jnp.where(irow == k, p_k, 0.0) + jnp.where(irow == TOP_K + k, gates[k], 0.0)
    info_ref[c * 2 * TOP_K:(c + 1) * 2 * TOP_K, :] = info
    n8_ref[c] = n8


def _mixer(x, mod, hist, weights, *, n_sub, sub_len, pos0):
    n_seq, seq_len, _ = x.shape
    has_hist = hist is not None
    n_chain = MIXER_CHAINS
    assert n_sub * sub_len == TM
    if has_hist:
        assert seq_len == sub_len and n_seq % (n_sub * n_chain) == 0
        blk_seq, blk_rows = n_sub * n_chain, sub_len
        grid = (n_seq // blk_seq, 1)
        xmap = lambda i, j: (i, 0, 0)
        step = lambda i, j: i
    else:
        assert n_sub == 1 and seq_len % (sub_len * n_chain) == 0
        blk_seq, blk_rows = 1, sub_len * n_chain
        steps_per_seq = seq_len // blk_rows
        grid = (n_seq, steps_per_seq)
        xmap = lambda i, j: (i, j, 0)
        step = lambda i, j: i * steps_per_seq + j
    n_tiles = grid[0] * grid[1] * n_chain

    def full(arr):
        nd = arr.ndim
        return pl.BlockSpec(arr.shape, lambda i, j, _nd=nd: (0,) * _nd, pipeline_mode=pl.Buffered(1))

    in_specs = [pl.BlockSpec((blk_seq, blk_rows, D_MODEL), xmap),
                pl.BlockSpec((blk_seq, N_MOD, D_MODEL), lambda i, j: (i, 0, 0))]
    args = [x, mod]
    if has_hist:
        in_specs.append(pl.BlockSpec((blk_seq, POOL_MAX - 1, POOL_WIDTH), lambda i, j: (i, 0, 0)))
        args.append(hist)
    in_specs += [full(w) for w in weights]
    args += list(weights)

    out_shape = [jax.ShapeDtypeStruct(x.shape, F32),
                 jax.ShapeDtypeStruct(x.shape, BF16),
                 jax.ShapeDtypeStruct((n_seq, POOL_MAX - 1, POOL_WIDTH), F32)]
    out_specs = [pl.BlockSpec((blk_seq, blk_rows, D_MODEL), xmap),
                 pl.BlockSpec((blk_seq, blk_rows, D_MODEL), xmap),
                 pl.BlockSpec((blk_seq, POOL_MAX - 1, POOL_WIDTH), lambda i, j: (i, 0, 0))]
    if has_hist:
        out_shape.append(jax.ShapeDtypeStruct((n_seq * seq_len, GMLP_WIDTH), F32))
        out_specs.append(pl.BlockSpec((n_chain * TM, GMLP_WIDTH), lambda i, j: (i, 0)))
    out_shape += [jax.ShapeDtypeStruct((n_tiles * 2 * TOP_K, TM), F32),
                  jax.ShapeDtypeStruct((n_tiles, N_EXPERTS, 1), jnp.int32)]
    out_specs += [pl.BlockSpec((n_chain * 2 * TOP_K, TM), lambda i, j: (step(i, j), 0)),
                  pl.BlockSpec((n_chain, N_EXPERTS, 1), lambda i, j: (step(i, j), 0, 0))]

    return pl.pallas_call(
        functools.partial(_mixer_body, n_chain, n_sub, sub_len, has_hist, pos0),
        grid=grid,
        in_specs=in_specs,
        out_specs=out_specs,
        out_shape=out_shape,
        scratch_shapes=[pltpu.VMEM((n_chain * n_sub, POOL_MAX + sub_len, POOL_WIDTH), F32)],
        compiler_params=pltpu.CompilerParams(dimension_semantics=("arbitrary", "arbitrary"),
                                             vmem_limit_bytes=VMEM_LIMIT),
        name="mixer_hist" if has_hist else "mixer",
    )(*args)


def _slot_matrix(info, weights):
    rows = lax.broadcasted_iota(jnp.int32, (RLOC, TM), 0)
    out = jnp.zeros((RLOC, TM), F32)
    for k in range(TOP_K):
        p_k = info[k:k + 1, :].astype(jnp.int32)
        out = out + jnp.where(p_k == rows, weights[k], 0.0)
    return out.astype(BF16)


def _run_copy(src_ref, src_row, dst_ref, dst_row, n_rows, sem):
    return pltpu.make_async_copy(src_ref.at[pl.ds(src_row, n_rows), :],
                                 dst_ref.at[pl.ds(dst_row, n_rows), :], sem)


def _dispatch_body(n_tiles, aliased, n8_s, off_s, base_s, tot_s, *refs):
    if aliased:
        h2_ref, info_ref, _xb_in, xb_ref, xs_ref, sems = refs
    else:
        h2_ref, info_ref, xb_ref, xs_ref, sems = refs
    i = pl.program_id(0)
    slot = i % 2

    def wait_tile(t, sl):
        tot = pl.multiple_of(tot_s[t], ROW_ALIGN)
        _run_copy(xs_ref.at[sl], 0, xb_ref, 0, tot, sems.at[sl]).wait()

    @pl.when(i >= 2)
    def _():
        wait_tile(i - 2, slot)

    info = info_ref[...]
    ones = [1.0] * TOP_K
    perm = _slot_matrix(info, ones)
    xs_ref[slot] = _dot(perm, h2_ref[...].reshape(TM, D_MODEL))
    for e in range(N_EXPERTS):
        n = pl.multiple_of(n8_s[i * N_EXPERTS + e], ROW_ALIGN)

        @pl.when(n > 0)
        def _(e=e, n=n):
            src = pl.multiple_of(off_s[i * N_EXPERTS + e], ROW_ALIGN)
            dst = pl.multiple_of(base_s[i * N_EXPERTS + e], ROW_ALIGN)
            _run_copy(xs_ref.at[slot], src, xb_ref, dst, n, sems.at[slot]).start()

    @pl.when(i == n_tiles - 1)
    def _():
        if n_tiles >= 2:
            wait_tile(i - 1, 1 - slot)
        wait_tile(i, slot)


def _dispatch(h2, info, tables, xb, *, n_sub, sub_len, n_rows):
    n_seq, seq_len, _ = h2.shape
    tiles_per_seq = seq_len // sub_len
    n_tiles = n_seq // n_sub * tiles_per_seq
    if n_sub == 1:
        hmap = lambda i, *_: (i // tiles_per_seq, i % tiles_per_seq, 0)
    else:
        hmap = lambda i, *_: (i, 0, 0)
    in_specs = [pl.BlockSpec((n_sub, sub_len, D_MODEL), hmap),
                pl.BlockSpec((2 * TOP_K, TM), lambda i, *_: (i, 0))]
    args = [h2, info]
    aliases = {}
    if xb is not None:
        in_specs.append(pl.BlockSpec(memory_space=pl.ANY))
        args.append(xb)
        aliases = {len(tables) + 2: 0}
    return pl.pallas_call(
        functools.partial(_dispatch_body, n_tiles, xb is not None),
        grid_spec=pltpu.PrefetchScalarGridSpec(
            num_scalar_prefetch=len(tables),
            grid=(n_tiles,),
            in_specs=in_specs,
            out_specs=pl.BlockSpec(memory_space=pl.ANY),
            scratch_shapes=[pltpu.VMEM((2, RLOC, D_MODEL), F32),
                            pltpu.SemaphoreType.DMA((2,))],
        ),
        out_shape=jax.ShapeDtypeStruct((n_rows, D_MODEL), F32),
        input_output_aliases=aliases,
        compiler_params=pltpu.CompilerParams(dimension_semantics=("arbitrary",),
                                             vmem_limit_bytes=VMEM_LIMIT),
        name="dispatch_more" if xb is not None else "dispatch",
    )(*tables, *args)


def _expert_rows(xv, wup_bf, bup, wdn_bf, bdn):
    z = _dot(xv.astype(BF16), wup_bf[...]) + bup
    zg = jnp.minimum(z[:, :D_FF], SWIGLU_LIMIT)
    zl = jnp.clip(z[:, D_FF:], -SWIGLU_LIMIT, SWIGLU_LIMIT)
    act = zg * _sigmoid(SWIGLU_ALPHA * zg) * (zl + 1.0)
    return _dot(act.astype(BF16), wdn_bf[...]) + bdn


def _expert_body(ends_s, row_s, rows_s, xb_ref, wup_ref, bup_ref, wdn_ref, bdn_ref, yb_ref, x_buf, y_buf,
                 wup_bf, wdn_bf, x_sem, y_sem):
    e = pl.program_id(0)
    n_total = ends_s[N_EXPERTS - 1]
    lo = jnp.where(e == 0, 0, ends_s[jnp.maximum(e - 1, 0)])
    hi = ends_s[e]

    def x_copy(item):
        n = pl.multiple_of(rows_s[item], ROW_ALIGN)
        row = pl.multiple_of(row_s[item], ROW_ALIGN)
        slot = item % X_SLOTS
        return _run_copy(xb_ref, row, x_buf.at[slot], 0, n, x_sem.at[slot])

    def y_copy(item):
        n = pl.multiple_of(rows_s[item], ROW_ALIGN)
        row = pl.multiple_of(row_s[item], ROW_ALIGN)
        slot = item % 2
        return _run_copy(y_buf.at[slot], 0, yb_ref, row, n, y_sem.at[slot])

    @pl.when(e == 0)
    def _():
        x_buf[...] = jnp.zeros(x_buf.shape, F32)
        for ahead in range(X_SLOTS - 1):
            @pl.when(ahead < n_total)
            def _(ahead=ahead):
                x_copy(ahead).start()

    @pl.when(hi > lo)
    def _():
        wup_bf[...] = wup_ref[...].astype(BF16)
        wdn_bf[...] = wdn_ref[...].astype(BF16)
        bup = bup_ref[...]
        bdn = bdn_ref[...]

        def run_item(item, carry):
            xs = item % X_SLOTS
            ys = item % 2
            x_copy(item).wait()

            @pl.when(item + (X_SLOTS - 1) < n_total)
            def _():
                x_copy(item + (X_SLOTS - 1)).start()

            @pl.when(item >= 2)
            def _():
                y_copy(item - 2).wait()

            half = BM // 2

            @pl.when(rows_s[item] > half)
            def _():
                y_buf[ys] = _expert_rows(x_buf[xs], wup_bf, bup, wdn_bf, bdn)

            @pl.when(rows_s[item] <= half)
            def _():
                y_buf[ys, 0:half, :] = _expert_rows(x_buf[xs, 0:half, :], wup_bf, bup, wdn_bf, bdn)

            y_copy(item).start()
            return carry

        lax.fori_loop(lo, hi, run_item, 0)

    @pl.when(e == N_EXPERTS - 1)
    def _():
        @pl.when(n_total >= 2)
        def _():
            y_copy(n_total - 2).wait()

        y_copy(n_total - 1).wait()


def _experts(xb, ends, item_row, item_rows, w_up, b_up, w_down, b_down):
    n_rows = xb.shape[0]
    wmap = lambda e, *_: (e, 0, 0)
    return pl.pallas_call(
        _expert_body,
        grid_spec=pltpu.PrefetchScalarGridSpec(
            num_scalar_prefetch=3,
            grid=(N_EXPERTS,),
            in_specs=[pl.BlockSpec(memory_space=pl.ANY),
                      pl.BlockSpec((None, D_MODEL, 2 * D_FF), wmap),
                      pl.BlockSpec((None, 1, 2 * D_FF), wmap),
                      pl.BlockSpec((None, D_FF, D_MODEL), wmap),
                      pl.BlockSpec((None, 1, D_MODEL), wmap)],
            out_specs=pl.BlockSpec(memory_space=pl.ANY),
            scratch_shapes=[pltpu.VMEM((X_SLOTS, BM, D_MODEL), F32),
                            pltpu.VMEM((2, BM, D_MODEL), F32),
                            pltpu.VMEM((D_MODEL, 2 * D_FF), BF16),
                            pltpu.VMEM((D_FF, D_MODEL), BF16),
                            pltpu.SemaphoreType.DMA((X_SLOTS,)),
                            pltpu.SemaphoreType.DMA((2,))],
        ),
        out_shape=jax.ShapeDtypeStruct((n_rows, D_MODEL), F32),
        compiler_params=pltpu.CompilerParams(dimension_semantics=("arbitrary",),
                                             vmem_limit_bytes=VMEM_LIMIT),
        name="experts",
    )(ends, item_row, item_rows, xb, w_up, b_up, w_down, b_down)


def _combine_body(n_tiles, n8_s, off_s, base_s, tot_s, yb_ref, info_ref, x1_ref, mod_ref, gpost_ref, o_ref,
                  yloc_ref, sems):
    i = pl.program_id(0)
    slot = i % 2

    def start_tile(t, sl):
        for e in range(N_EXPERTS):
            n = pl.multiple_of(n8_s[t * N_EXPERTS + e], ROW_ALIGN)

            @pl.when(n > 0)
            def _(e=e, n=n):
                src = pl.multiple_of(base_s[t * N_EXPERTS + e], ROW_ALIGN)
                dst = pl.multiple_of(off_s[t * N_EXPERTS + e], ROW_ALIGN)
                _run_copy(yb_ref, src, yloc_ref.at[sl], dst, n, sems.at[sl]).start()

    @pl.when(i == 0)
    def _():
        yloc_ref[:, TM * TOP_K:RLOC, :] = jnp.zeros((2, RLOC - TM * TOP_K, D_MODEL), F32)
        start_tile(0, 0)

    @pl.when(i + 1 < n_tiles)
    def _():
        start_tile(i + 1, 1 - slot)

    tot = pl.multiple_of(tot_s[i], ROW_ALIGN)
    _run_copy(yb_ref, 0, yloc_ref.at[slot], 0, tot, sems.at[slot]).wait()

    info = info_ref[...]
    wmat = _slot_matrix(info, [info[TOP_K + k:TOP_K + k + 1, :] for k in range(TOP_K)])
    f = lax.dot_general(wmat, yloc_ref[slot].astype(BF16), (((0,), (0,)), ((), ())),
                        preferred_element_type=F32)
    x1 = x1_ref[...]
    gate_f = mod_ref[...][:, 5:6]
    o_ref[...] = x1 + gate_f * _rms(f.reshape(x1.shape), gpost_ref[...])


def _combine(yb, info, x1, mod, g_post, tables, *, n_sub, sub_len):
    n_seq, seq_len, _ = x1.shape
    tiles_per_seq = seq_len // sub_len
    n_tiles = n_seq // n_sub * tiles_per_seq
    if n_sub == 1:
        xmap = lambda i, *_: (i // tiles_per_seq, i % tiles_per_seq, 0)
        mmap = lambda i, *_: (i // tiles_per_seq, 0, 0)
    else:
        xmap = lambda i, *_: (i, 0, 0)
        mmap = lambda i, *_: (i, 0, 0)
    return pl.pallas_call(
        functools.partial(_combine_body, n_tiles),
        grid_spec=pltpu.PrefetchScalarGridSpec(
            num_scalar_prefetch=len(tables),
            grid=(n_tiles,),
            in_specs=[pl.BlockSpec(memory_space=pl.ANY),
                      pl.BlockSpec((2 * TOP_K, TM), lambda i, *_: (i, 0)),
                      pl.BlockSpec((n_sub, sub_len, D_MODEL), xmap),
                      pl.BlockSpec((n_sub, N_MOD, D_MODEL), mmap),
                      pl.BlockSpec((1, D_MODEL), lambda i, *_: (0, 0))],
            out_specs=pl.BlockSpec((n_sub, sub_len, D_MODEL), xmap),
            scratch_shapes=[pltpu.VMEM((2, RLOC, D_MODEL), F32),
                            pltpu.SemaphoreType.DMA((2,))],
        ),
        out_shape=jax.ShapeDtypeStruct(x1.shape, F32),
        compiler_params=pltpu.CompilerParams(dimension_semantics=("arbitrary",),
                                             vmem_limit_bytes=VMEM_LIMIT),
        name="combine",
    )(*tables, yb, info, x1, mod, g_post)


def _tables(n8_all):
    n_tiles = n8_all.shape[0]
    off = jnp.cumsum(n8_all, axis=1) - n8_all
    tot = jnp.sum(n8_all, axis=1)
    cnt = jnp.sum(n8_all, axis=0)
    start = jnp.cumsum(cnt) - cnt
    base = start[None, :] + jnp.cumsum(n8_all, axis=0) - n8_all
    items = (cnt + BM - 1) // BM
    ends = jnp.cumsum(items)
    return dict(n8=n8_all, off=off, base=base, tot=tot, cnt=cnt, start=start, items=items, ends=ends)


def kernel(x_prompt, x_sample, cache_pool, c_prompt, c_sample, w_ada, b_ada, g_pre_mix, g_post_mix, g_pre_ffn, g_post_ffn, w_in, w_pool, s_pool, g_v, b_v, w_s, b_s, p_a, p_b, w_o, w_router, b_router, w_up, b_up, w_down, b_down):
    assert w_ada.shape[0] == 1, "single-layer trunk"
    n_p, n_s = x_prompt.shape[0], x_sample.shape[0]
    past_len = x_prompt.shape[1]
    i32 = jnp.int32

    mod = _ada(jnp.concatenate([c_prompt, c_sample], axis=0), w_ada.reshape(D_MODEL, N_MOD * D_MODEL), b_ada)
    mod = mod.reshape(n_p + n_s, N_MOD, D_MODEL)
    mod_p, mod_s = mod[:n_p], mod[n_p:]

    row = lambda v: v.reshape(1, -1)
    weights = (row(g_pre_mix[0]), w_in[0].astype(BF16), w_pool[0].astype(BF16), row(s_pool[0]), row(g_v[0]),
               row(b_v[0]), w_s[0].astype(BF16), b_s[0].T, p_a[0].astype(BF16), p_b[0].astype(BF16),
               w_o[0].astype(BF16), row(g_post_mix[0]), row(g_pre_ffn[0]), w_router[0].T.astype(BF16),
               b_router[0].reshape(-1, 1))

    sub_s = x_sample.shape[1]
    x1_p, h2_p, pool_p, info_p, n8_p = _mixer(x_prompt, mod_p, None, weights, n_sub=1, sub_len=TM, pos0=0)
    x1_s, h2_s, pool_s, v_s, info_s, n8_s = _mixer(x_sample, mod_s, cache_pool[0], weights,
                                                   n_sub=TM // sub_s, sub_len=sub_s, pos0=past_len)

    tiles_p, tiles_s = n8_p.shape[0], n8_s.shape[0]
    n_tiles = tiles_p + tiles_s
    tb = _tables(jnp.concatenate([n8_p, n8_s], axis=0).reshape(n_tiles, N_EXPERTS))
    n_rows = n_tiles * (TM * TOP_K + N_EXPERTS * (ROW_ALIGN - 1))
    n_items = n_rows // BM + N_EXPERTS

    def tile_tables(lo, hi):
        return (tb["n8"][lo:hi].reshape(-1).astype(i32), tb["off"][lo:hi].reshape(-1).astype(i32),
                tb["base"][lo:hi].reshape(-1).astype(i32), tb["tot"][lo:hi].astype(i32))

    xb = _dispatch(h2_p, info_p, tile_tables(0, tiles_p), None, n_sub=1, sub_len=TM, n_rows=n_rows)
    xb = _dispatch(h2_s, info_s, tile_tables(tiles_p, n_tiles), xb,
                   n_sub=TM // sub_s, sub_len=sub_s, n_rows=n_rows)

    item = jnp.arange(n_items, dtype=i32)[:, None]
    first = (tb["ends"] - tb["items"])[None, :]
    mine = ((first <= item) & (item < tb["ends"][None, :])).astype(i32)
    done = (item - first) * BM
    item_row = jnp.sum(mine * (tb["start"][None, :] + done), axis=1).astype(i32)
    item_rows = jnp.clip(jnp.sum(mine * (tb["cnt"][None, :] - done), axis=1), 0, BM).astype(i32)
    yb = _experts(xb, tb["ends"].astype(i32), item_row, item_rows, w_up.reshape(N_EXPERTS, D_MODEL, 2 * D_FF),
                  b_up.reshape(N_EXPERTS, 1, 2 * D_FF), w_down.reshape(N_EXPERTS, D_FF, D_MODEL),
                  b_down.reshape(N_EXPERTS, 1, D_MODEL))

    g_post = row(g_post_ffn[0])
    y_p = _combine(yb, info_p, x1_p, mod_p, g_post, tile_tables(0, tiles_p), n_sub=1, sub_len=TM)
    y_s = _combine(yb, info_s, x1_s, mod_s, g_post, tile_tables(tiles_p, n_tiles),
                   n_sub=TM // sub_s, sub_len=sub_s)

    return (y_p, y_s, pool_p[None], pool_s[None], v_s.reshape(1, n_s, sub_s, GMLP_WIDTH))
```

```python
import functools

import jax
import jax.numpy as jnp
from jax import lax
from jax.experimental import pallas as pl
from jax.experimental.pallas import tpu as pltpu

D_MODEL = 1024
POOL_WINDOWS = (2, 4, 8, 16)
POOL_MAX = 16
POOL_GDIM = 128
POOL_WIDTH = 512
GMLP_HEADS = 4
GMLP_CHUNK = 128
GMLP_WIDTH = 512
N_EXPERTS = 32
TOP_K = 4
D_FF = 1024
SWIGLU_LIMIT = 7.0
SWIGLU_ALPHA = 1.702
NORM_EPS = 1e-6
LN_EPS = 1e-5
N_MOD = 6

TM = 256
ROW_ALIGN = 8
RLOC = 1280
BM = 512
X_SLOTS = 3
MIXER_CHAINS = 4
VMEM_LIMIT = 56 * 1024 * 1024
ROW_WORDS = D_MODEL // 2
ROW_DTYPE = jnp.uint32

BF16 = jnp.bfloat16
F32 = jnp.float32


def _dot(a, b):
    return jnp.dot(a, b, preferred_element_type=F32)


def _rms(x, g):
    ms = jnp.mean(x * x, axis=-1, keepdims=True)
    return x * lax.rsqrt(ms + NORM_EPS) * g


def _sigmoid(x):
    return 0.5 * jnp.tanh(0.5 * x) + 0.5


def _gelu(x):
    return 0.5 * x * (1.0 + lax.erf(x * 0.7071067811865476))


def _ada_body(c_ref, w_ref, b_ref, o_ref):
    c = c_ref[...]
    s = c * jax.nn.sigmoid(c)
    o_ref[...] = _dot(s.astype(BF16), w_ref[...].astype(BF16)) + b_ref[...]


def _ada(c_all, w_ada, b_ada):
    n = c_all.shape[0]
    nt = N_MOD * D_MODEL // D_MODEL
    return pl.pallas_call(
        _ada_body,
        grid=(nt,),
        in_specs=[pl.BlockSpec((n, D_MODEL), lambda i: (0, 0)),
                  pl.BlockSpec((D_MODEL, D_MODEL), lambda i: (0, i)),
                  pl.BlockSpec((1, D_MODEL), lambda i: (0, i))],
        out_specs=pl.BlockSpec((n, D_MODEL), lambda i: (0, i)),
        out_shape=jax.ShapeDtypeStruct((n, N_MOD * D_MODEL), F32),
        compiler_params=pltpu.CompilerParams(dimension_semantics=("arbitrary",)),
        name="ada",
    )(c_all, w_ada, b_ada)


def _mixer_body(n_chain, n_sub, sub_len, has_hist, pos0, *refs):
    tiles = [_mixer_tile(c, n_chain, n_sub, sub_len, has_hist, pos0, *refs) for c in range(n_chain)]
    live = [True] * n_chain
    t = 0
    while any(live):
        for c, tile in enumerate(tiles):
            if live[c] and t >= c:
                live[c] = next(tile, None) is not None
        t += 1


def _mixer_tile(c, n_chain, n_sub, sub_len, has_hist, pos0, *refs):
    it = iter(refs)
    x_ref = next(it)
    mod_ref = next(it)
    hist_ref = next(it) if has_hist else None
    (gpre_ref, win_ref, wpool_ref, spool_ref, gv_ref, bv_ref, ws_ref, bst_ref, pa_ref, pb_ref, wo_ref,
     gpost_ref, gffn_ref, wr_ref, br_ref) = [next(it) for _ in range(15)]
    x1_ref = next(it)
    h2_ref = next(it)
    pool_ref = next(it)
    v_ref = next(it) if has_hist else None
    info_ref = next(it)
    n8_ref = next(it)
    ext_ref = next(it)

    if has_hist:
        seqs, rows = slice(c * n_sub, (c + 1) * n_sub), slice(None)
        mod = mod_ref[seqs]
    else:
        seqs, rows = slice(None), slice(c * sub_len, (c + 1) * sub_len)
        mod = mod_ref[...]
    x = x_ref[seqs, rows, :]
    shift_m, scale_m, gate_m = mod[:, 0:1], mod[:, 1:2], mod[:, 2:3]
    shift_f, scale_f = mod[:, 3:4], mod[:, 4:5]

    h = _rms(x, gpre_ref[...]) * (1.0 + scale_m) + shift_m
    hb = h.reshape(TM, D_MODEL).astype(BF16)

    a = _dot(hb, win_ref[:, 0:POOL_WIDTH])
    yield True
    e0 = c * n_sub
    if has_hist:
        for s in range(n_sub):
            ext_ref[e0 + s, 0:1, :] = jnp.zeros((1, POOL_WIDTH), F32)
            ext_ref[e0 + s, 1:POOL_MAX, :] = hist_ref[e0 + s]
            ext_ref[e0 + s, POOL_MAX:POOL_MAX + sub_len, :] = a[s * sub_len:(s + 1) * sub_len]
            pool_ref[e0 + s] = a[(s + 1) * sub_len - (POOL_MAX - 1):(s + 1) * sub_len]
        row0 = pos0
    else:
        j = pl.program_id(1)
        prev = (c - 1) % n_chain
        if c == 0:
            @pl.when(j == 0)
            def _():
                ext_ref[0, 0:POOL_MAX, :] = jnp.zeros((POOL_MAX, POOL_WIDTH), F32)

            @pl.when(j > 0)
            def _():
                ext_ref[0, 0:POOL_MAX, :] = ext_ref[prev, sub_len:sub_len + POOL_MAX, :]
        else:
            ext_ref[c, 0:POOL_MAX, :] = ext_ref[prev, sub_len:sub_len + POOL_MAX, :]

        ext_ref[c, POOL_MAX:POOL_MAX + sub_len, :] = a
        if c == n_chain - 1:
            pool_ref[0] = a[sub_len - (POOL_MAX - 1):sub_len]
        row0 = pos0 + (j * n_chain + c) * sub_len

    pos = row0 + lax.broadcasted_iota(jnp.int32, (sub_len, 1), 0)
    spool = spool_ref[...]
    ya_rows = []
    for s in range(n_sub):
        ya_groups = []
        for g, w in enumerate(POOL_WINDOWS):
            lanes = slice(g * POOL_GDIM, (g + 1) * POOL_GDIM)
            win = ext_ref[e0 + s, POOL_MAX:POOL_MAX + sub_len, lanes]
            for back in range(1, w):
                win = win + ext_ref[e0 + s, POOL_MAX - back:POOL_MAX - back + sub_len, lanes]
            cnt = jnp.minimum(pos + 1, w).astype(F32)
            dlt = win / cnt - a[s * sub_len:(s + 1) * sub_len, lanes]
            ya_groups.append(_dot(dlt.astype(BF16), wpool_ref[g]) * spool[:, lanes])
        ya_rows.append(jnp.concatenate(ya_groups, axis=1))
    y_a = jnp.concatenate(ya_rows, axis=0) if n_sub > 1 else ya_rows[0]
    yield True

    u = _dot(hb, win_ref[:, POOL_WIDTH:POOL_WIDTH + GMLP_WIDTH])
    v = _dot(hb, win_ref[:, POOL_WIDTH + GMLP_WIDTH:POOL_WIDTH + 2 * GMLP_WIDTH])
    yield True
    u = _gelu(u)
    v = _gelu(v)
    mu = jnp.mean(v, axis=-1, keepdims=True)
    vc = v - mu
    var = jnp.mean(vc * vc, axis=-1, keepdims=True)
    vn = vc * lax.rsqrt(var + LN_EPS) * gv_ref[...] + bv_ref[...]
    if has_hist:
        v_ref[c * TM:(c + 1) * TM, :] = vn
    vb = vn.astype(BF16)
    seg = min(sub_len, GMLP_CHUNK)
    tri = (lax.broadcasted_iota(jnp.int32, (seg, seg), 0) >= lax.broadcasted_iota(jnp.int32, (seg, seg), 1))
    bst = bst_ref[...]
    yb_rows = []
    for q in range(TM // seg):
        srows = slice(q * seg, (q + 1) * seg)
        yb_heads = []
        for g in range(GMLP_HEADS):
            lanes = slice(g * 128, (g + 1) * 128)
            wm = jnp.where(tri, ws_ref[g, 0:seg, 0:seg], jnp.zeros((), BF16))
            sg = _dot(wm, vb[srows, lanes]) + bst[0:seg, g:g + 1]
            yb_heads.append(u[srows, lanes] * sg)
        yb_rows.append(jnp.concatenate(yb_heads, axis=1))
    y_b = jnp.concatenate(yb_rows, axis=0)
    yield True

    c0 = POOL_WIDTH + 2 * GMLP_WIDTH
    ga = _dot(hb, win_ref[:, c0:c0 + D_MODEL])
    gb = _dot(hb, win_ref[:, c0 + D_MODEL:c0 + 2 * D_MODEL])
    yield True
    ma = _dot(y_a.astype(BF16), pa_ref[...])
    mb = _dot(y_b.astype(BF16), pb_ref[...])
    yield True
    m = _sigmoid(ga) * ma + _sigmoid(gb) * mb
    yield True
    mo = _dot(m.astype(BF16), wo_ref[...]).reshape(n_sub, sub_len, D_MODEL)
    yield True
    x1 = x + gate_m * _rms(mo, gpost_ref[...])
    x1_ref[seqs, rows, :] = x1

    h2 = (_rms(x1, gffn_ref[...]) * (1.0 + scale_f) + shift_f).astype(BF16)
    h2_ref[seqs, rows, :] = h2
    yield True
    logits = lax.dot_general(wr_ref[...], h2.reshape(TM, D_MODEL), (((1,), (1,)), ((), ())),
                             preferred_element_type=F32) + br_ref[...]

    erow = lax.broadcasted_iota(jnp.int32, (N_EXPERTS, TM), 0)
    work = logits
    onehots, vals = [], []
    for _ in range(TOP_K):
        mx = jnp.max(work, axis=0, keepdims=True)
        idx = jnp.min(jnp.where(work == mx, erow, N_EXPERTS), axis=0, keepdims=True)
        oh = erow == idx
        onehots.append(oh)
        vals.append(mx)
        work = jnp.where(oh, -jnp.inf, work)
    exps = [jnp.exp(vk - vals[0]) for vk in vals]
    den = exps[0] + exps[1] + exps[2] + exps[3]
    gates = [ek / den for ek in exps]

    sel = jnp.zeros((N_EXPERTS, TM), F32)
    for oh in onehots:
        sel = sel + oh.astype(F32)
    before = (lax.broadcasted_iota(jnp.int32, (TM, TM), 0) < lax.broadcasted_iota(jnp.int32, (TM, TM), 1))
    rank = _dot(sel.astype(BF16), before.astype(BF16))
    n_e = jnp.sum(sel, axis=1, keepdims=True).astype(jnp.int32)
    n8 = ((n_e + (ROW_ALIGN - 1)) // ROW_ALIGN) * ROW_ALIGN
    lower = (lax.broadcasted_iota(jnp.int32, (N_EXPERTS, N_EXPERTS), 0)
             > lax.broadcasted_iota(jnp.int32, (N_EXPERTS, N_EXPERTS), 1))
    n8b = jnp.broadcast_to(n8.astype(F32), (N_EXPERTS, 128)).astype(BF16)
    off = _dot(lower.astype(BF16), n8b)[:, 0:1]
    slot = off + rank
    irow = lax.broadcasted_iota(jnp.int32, (2 * TOP_K, TM), 0)
    info = jnp.zeros((2 * TOP_K, TM), F32)
    for k in range(TOP_K):
        p_k = jnp.sum(jnp.where(onehots[k], slot, 0.0), axis=0, keepdims=True)
        info = info + jnp.where(irow == k, p_k, 0.0) + jnp.where(irow == TOP_K + k, gates[k], 0.0)
    info_ref[c * 2 * TOP_K:(c + 1) * 2 * TOP_K, :] = info
    n8_ref[c] = n8


def _mixer(x, mod, hist, weights, *, n_sub, sub_len, pos0):
    n_seq, seq_len, _ = x.shape
    has_hist = hist is not None
    n_chain = MIXER_CHAINS
    assert n_sub * sub_len == TM
    if has_hist:
        assert seq_len == sub_len and n_seq % (n_sub * n_chain) == 0
        blk_seq, blk_rows = n_sub * n_chain, sub_len
        grid = (n_seq // blk_seq, 1)
        xmap = lambda i, j: (i, 0, 0)
        step = lambda i, j: i
    else:
        assert n_sub == 1 and seq_len % (sub_len * n_chain) == 0
        blk_seq, blk_rows = 1, sub_len * n_chain
        steps_per_seq = seq_len // blk_rows
        grid = (n_seq, steps_per_seq)
        xmap = lambda i, j: (i, j, 0)
        step = lambda i, j: i * steps_per_seq + j
    n_tiles = grid[0] * grid[1] * n_chain

    def full(arr):
        nd = arr.ndim
        return pl.BlockSpec(arr.shape, lambda i, j, _nd=nd: (0,) * _nd, pipeline_mode=pl.Buffered(1))

    in_specs = [pl.BlockSpec((blk_seq, blk_rows, D_MODEL), xmap),
                pl.BlockSpec((blk_seq, N_MOD, D_MODEL), lambda i, j: (i, 0, 0))]
    args = [x, mod]
    if has_hist:
        in_specs.append(pl.BlockSpec((blk_seq, POOL_MAX - 1, POOL_WIDTH), lambda i, j: (i, 0, 0)))
        args.append(hist)
    in_specs += [full(w) for w in weights]
    args += list(weights)

    out_shape = [jax.ShapeDtypeStruct(x.shape, F32),
                 jax.ShapeDtypeStruct(x.shape, BF16),
                 jax.ShapeDtypeStruct((n_seq, POOL_MAX - 1, POOL_WIDTH), F32)]
    out_specs = [pl.BlockSpec((blk_seq, blk_rows, D_MODEL), xmap),
                 pl.BlockSpec((blk_seq, blk_rows, D_MODEL), xmap),
                 pl.BlockSpec((blk_seq, POOL_MAX - 1, POOL_WIDTH), lambda i, j: (i, 0, 0))]
    if has_hist:
        out_shape.append(jax.ShapeDtypeStruct((n_seq * seq_len, GMLP_WIDTH), F32))
        out_specs.append(pl.BlockSpec((n_chain * TM, GMLP_WIDTH), lambda i, j: (i, 0)))
    out_shape += [jax.ShapeDtypeStruct((n_tiles * 2 * TOP_K, TM), F32),
                  jax.ShapeDtypeStruct((n_tiles, N_EXPERTS, 1), jnp.int32)]
    out_specs += [pl.BlockSpec((n_chain * 2 * TOP_K, TM), lambda i, j: (step(i, j), 0)),
                  pl.BlockSpec((n_chain, N_EXPERTS, 1), lambda i, j: (step(i, j), 0, 0))]

    return pl.pallas_call(
        functools.partial(_mixer_body, n_chain, n_sub, sub_len, has_hist, pos0),
        grid=grid,
        in_specs=in_specs,
        out_specs=out_specs,
        out_shape=out_shape,
        scratch_shapes=[pltpu.VMEM((n_chain * n_sub, POOL_MAX + sub_len, POOL_WIDTH), F32)],
        compiler_params=pltpu.CompilerParams(dimension_semantics=("arbitrary", "arbitrary"),
                                             vmem_limit_bytes=VMEM_LIMIT),
        name="mixer_hist" if has_hist else "mixer",
    )(*args)


def _slot_matrix(info, weights):
    rows = lax.broadcasted_iota(jnp.int32, (RLOC, TM), 0)
    out = jnp.zeros((RLOC, TM), F32)
    for k in range(TOP_K):
        p_k = info[k:k + 1, :].astype(jnp.int32)
        out = out + jnp.where(p_k == rows, weights[k], 0.0)
    return out.astype(BF16)


def _run_copy(src_ref, src_row, dst_ref, dst_row, n_rows, sem):
    return pltpu.make_async_copy(src_ref.at[pl.ds(src_row, n_rows), :],
                                 dst_ref.at[pl.ds(dst_row, n_rows), :], sem)


def _pack_rows(x):
    half = x.shape[1] // 2
    return pltpu.pack_elementwise([x[:, :half], x[:, half:]], packed_dtype=BF16)


def _unpack_rows(w):
    lo = pltpu.unpack_elementwise(w, index=0, packed_dtype=BF16, unpacked_dtype=F32)
    hi = pltpu.unpack_elementwise(w, index=1, packed_dtype=BF16, unpacked_dtype=F32)
    return jnp.concatenate([lo, hi], axis=1).astype(BF16)


def _dispatch_body(n_tiles, aliased, n8_s, off_s, base_s, tot_s, *refs):
    if aliased:
        h2_ref, info_ref, _xb_in, xb_ref, xs_ref, sems = refs
    else:
        h2_ref, info_ref, xb_ref, xs_ref, sems = refs
    i = pl.program_id(0)
    slot = i % 2

    def wait_tile(t, sl):
        tot = pl.multiple_of(tot_s[t], ROW_ALIGN)
        _run_copy(xs_ref.at[sl], 0, xb_ref, 0, tot, sems.at[sl]).wait()

    @pl.when(i >= 2)
    def _():
        wait_tile(i - 2, slot)

    info = info_ref[...]
    ones = [1.0] * TOP_K
    perm = _slot_matrix(info, ones)
    xs_ref[slot] = _pack_rows(_dot(perm, h2_ref[...].reshape(TM, D_MODEL)))
    for e in range(N_EXPERTS):
        n = pl.multiple_of(n8_s[i * N_EXPERTS + e], ROW_ALIGN)

        @pl.when(n > 0)
        def _(e=e, n=n):
            src = pl.multiple_of(off_s[i * N_EXPERTS + e], ROW_ALIGN)
            dst = pl.multiple_of(base_s[i * N_EXPERTS + e], ROW_ALIGN)
            _run_copy(xs_ref.at[slot], src, xb_ref, dst, n, sems.at[slot]).start()

    @pl.when(i == n_tiles - 1)
    def _():
        if n_tiles >= 2:
            wait_tile(i - 1, 1 - slot)
        wait_tile(i, slot)


def _dispatch(h2, info, tables, xb, *, n_sub, sub_len, n_rows):
    n_seq, seq_len, _ = h2.shape
    tiles_per_seq = seq_len // sub_len
    n_tiles = n_seq // n_sub * tiles_per_seq
    if n_sub == 1:
        hmap = lambda i, *_: (i // tiles_per_seq, i % tiles_per_seq, 0)
    else:
        hmap = lambda i, *_: (i, 0, 0)
    in_specs = [pl.BlockSpec((n_sub, sub_len, D_MODEL), hmap),
                pl.BlockSpec((2 * TOP_K, TM), lambda i, *_: (i, 0))]
    args = [h2, info]
    aliases = {}
    if xb is not None:
        in_specs.append(pl.BlockSpec(memory_space=pl.ANY))
        args.append(xb)
        aliases = {len(tables) + 2: 0}
    return pl.pallas_call(
        functools.partial(_dispatch_body, n_tiles, xb is not None),
        grid_spec=pltpu.PrefetchScalarGridSpec(
            num_scalar_prefetch=len(tables),
            grid=(n_tiles,),
            in_specs=in_specs,
            out_specs=pl.BlockSpec(memory_space=pl.ANY),
            scratch_shapes=[pltpu.VMEM((2, RLOC, ROW_WORDS), ROW_DTYPE),
                            pltpu.SemaphoreType.DMA((2,))],
        ),
        out_shape=jax.ShapeDtypeStruct((n_rows, ROW_WORDS), ROW_DTYPE),
        input_output_aliases=aliases,
        compiler_params=pltpu.CompilerParams(dimension_semantics=("arbitrary",),
                                             vmem_limit_bytes=VMEM_LIMIT),
        name="dispatch_more" if xb is not None else "dispatch",
    )(*tables, *args)


def _expert_rows(xv, wup_bf, bup, wdn_bf, bdn):
    z = _dot(_unpack_rows(xv), wup_bf[...]) + bup
    zg = jnp.minimum(z[:, :D_FF], SWIGLU_LIMIT)
    zl = jnp.clip(z[:, D_FF:], -SWIGLU_LIMIT, SWIGLU_LIMIT)
    act = zg * _sigmoid(SWIGLU_ALPHA * zg) * (zl + 1.0)
    return _pack_rows(_dot(act.astype(BF16), wdn_bf[...]) + bdn)


def _expert_body(ends_s, row_s, rows_s, xb_ref, wup_ref, bup_ref, wdn_ref, bdn_ref, yb_ref, x_buf, y_buf,
                 wup_bf, wdn_bf, x_sem, y_sem):
    e = pl.program_id(0)
    n_total = ends_s[N_EXPERTS - 1]
    lo = jnp.where(e == 0, 0, ends_s[jnp.maximum(e - 1, 0)])
    hi = ends_s[e]

    def x_copy(item):
        n = pl.multiple_of(rows_s[item], ROW_ALIGN)
        row = pl.multiple_of(row_s[item], ROW_ALIGN)
        slot = item % X_SLOTS
        return _run_copy(xb_ref, row, x_buf.at[slot], 0, n, x_sem.at[slot])

    def y_copy(item):
        n = pl.multiple_of(rows_s[item], ROW_ALIGN)
        row = pl.multiple_of(row_s[item], ROW_ALIGN)
        slot = item % 2
        return _run_copy(y_buf.at[slot], 0, yb_ref, row, n, y_sem.at[slot])

    @pl.when(e == 0)
    def _():
        x_buf[...] = jnp.zeros(x_buf.shape, ROW_DTYPE)
        for ahead in range(X_SLOTS - 1):
            @pl.when(ahead < n_total)
            def _(ahead=ahead):
                x_copy(ahead).start()

    @pl.when(hi > lo)
    def _():
        wup_bf[...] = wup_ref[...].astype(BF16)
        wdn_bf[...] = wdn_ref[...].astype(BF16)
        bup = bup_ref[...]
        bdn = bdn_ref[...]

        def run_item(item, carry):
            xs = item % X_SLOTS
            ys = item % 2
            x_copy(item).wait()

            @pl.when(item + (X_SLOTS - 1) < n_total)
            def _():
                x_copy(item + (X_SLOTS - 1)).start()

            @pl.when(item >= 2)
            def _():
                y_copy(item - 2).wait()

            half = BM // 2

            @pl.when(rows_s[item] > half)
            def _():
                y_buf[ys] = _expert_rows(x_buf[xs], wup_bf, bup, wdn_bf, bdn)

            @pl.when(rows_s[item] <= half)
            def _():
                y_buf[ys, 0:half, :] = _expert_rows(x_buf[xs, 0:half, :], wup_bf, bup, wdn_bf, bdn)

            y_copy(item).start()
            return carry

        lax.fori_loop(lo, hi, run_item, 0)

    @pl.when(e == N_EXPERTS - 1)
    def _():
        @pl.when(n_total >= 2)
        def _():
            y_copy(n_total - 2).wait()

        y_copy(n_total - 1).wait()


def _experts(xb, ends, item_row, item_rows, w_up, b_up, w_down, b_down):
    n_rows = xb.shape[0]
    wmap = lambda e, *_: (e, 0, 0)
    return pl.pallas_call(
        _expert_body,
        grid_spec=pltpu.PrefetchScalarGridSpec(
            num_scalar_prefetch=3,
            grid=(N_EXPERTS,),
            in_specs=[pl.BlockSpec(memory_space=pl.ANY),
                      pl.BlockSpec((None, D_MODEL, 2 * D_FF), wmap),
                      pl.BlockSpec((None, 1, 2 * D_FF), wmap),
                      pl.BlockSpec((None, D_FF, D_MODEL), wmap),
                      pl.BlockSpec((None, 1, D_MODEL), wmap)],
            out_specs=pl.BlockSpec(memory_space=pl.ANY),
            scratch_shapes=[pltpu.VMEM((X_SLOTS, BM, ROW_WORDS), ROW_DTYPE),
                            pltpu.VMEM((2, BM, ROW_WORDS), ROW_DTYPE),
                            pltpu.VMEM((D_MODEL, 2 * D_FF), BF16),
                            pltpu.VMEM((D_FF, D_MODEL), BF16),
                            pltpu.SemaphoreType.DMA((X_SLOTS,)),
                            pltpu.SemaphoreType.DMA((2,))],
        ),
        out_shape=jax.ShapeDtypeStruct((n_rows, ROW_WORDS), ROW_DTYPE),
        compiler_params=pltpu.CompilerParams(dimension_semantics=("arbitrary",),
                                             vmem_limit_bytes=VMEM_LIMIT),
        name="experts",
    )(ends, item_row, item_rows, xb, w_up, b_up, w_down, b_down)


def _combine_body(n_tiles, n8_s, off_s, base_s, tot_s, yb_ref, info_ref, x1_ref, mod_ref, gpost_ref, o_ref,
                  yloc_ref, sems):
    i = pl.program_id(0)
    slot = i % 2

    def start_tile(t, sl):
        for e in range(N_EXPERTS):
            n = pl.multiple_of(n8_s[t * N_EXPERTS + e], ROW_ALIGN)

            @pl.when(n > 0)
            def _(e=e, n=n):
                src = pl.multiple_of(base_s[t * N_EXPERTS + e], ROW_ALIGN)
                dst = pl.multiple_of(off_s[t * N_EXPERTS + e], ROW_ALIGN)
                _run_copy(yb_ref, src, yloc_ref.at[sl], dst, n, sems.at[sl]).start()

    @pl.when(i == 0)
    def _():
        yloc_ref[:, TM * TOP_K:RLOC, :] = jnp.zeros((2, RLOC - TM * TOP_K, ROW_WORDS), ROW_DTYPE)
        start_tile(0, 0)

    @pl.when(i + 1 < n_tiles)
    def _():
        start_tile(i + 1, 1 - slot)

    tot = pl.multiple_of(tot_s[i], ROW_ALIGN)
    _run_copy(yb_ref, 0, yloc_ref.at[slot], 0, tot, sems.at[slot]).wait()

    info = info_ref[...]
    wmat = _slot_matrix(info, [info[TOP_K + k:TOP_K + k + 1, :] for k in range(TOP_K)])
    f = lax.dot_general(wmat, _unpack_rows(yloc_ref[slot]), (((0,), (0,)), ((), ())),
                        preferred_element_type=F32)
    x1 = x1_ref[...]
    gate_f = mod_ref[...][:, 5:6]
    o_ref[...] = x1 + gate_f * _rms(f.reshape(x1.shape), gpost_ref[...])


def _combine(yb, info, x1, mod, g_post, tables, *, n_sub, sub_len):
    n_seq, seq_len, _ = x1.shape
    tiles_per_seq = seq_len // sub_len
    n_tiles = n_seq // n_sub * tiles_per_seq
    if n_sub == 1:
        xmap = lambda i, *_: (i // tiles_per_seq, i % tiles_per_seq, 0)
        mmap = lambda i, *_: (i // tiles_per_seq, 0, 0)
    else:
        xmap = lambda i, *_: (i, 0, 0)
        mmap = lambda i, *_: (i, 0, 0)
    return pl.pallas_call(
        functools.partial(_combine_body, n_tiles),
        grid_spec=pltpu.PrefetchScalarGridSpec(
            num_scalar_prefetch=len(tables),
            grid=(n_tiles,),
            in_specs=[pl.BlockSpec(memory_space=pl.ANY),
                      pl.BlockSpec((2 * TOP_K, TM), lambda i, *_: (i, 0)),
                      pl.BlockSpec((n_sub, sub_len, D_MODEL), xmap),
                      pl.BlockSpec((n_sub, N_MOD, D_MODEL), mmap),
                      pl.BlockSpec((1, D_MODEL), lambda i, *_: (0, 0))],
            out_specs=pl.BlockSpec((n_sub, sub_len, D_MODEL), xmap),
            scratch_shapes=[pltpu.VMEM((2, RLOC, ROW_WORDS), ROW_DTYPE),
                            pltpu.SemaphoreType.DMA((2,))],
        ),
        out_shape=jax.ShapeDtypeStruct(x1.shape, F32),
        compiler_params=pltpu.CompilerParams(dimension_semantics=("arbitrary",),
                                             vmem_limit_bytes=VMEM_LIMIT),
        name="combine",
    )(*tables, yb, info, x1, mod, g_post)


def _tables(n8_all):
    n_tiles = n8_all.shape[0]
    off = jnp.cumsum(n8_all, axis=1) - n8_all
    tot = jnp.sum(n8_all, axis=1)
    cnt = jnp.sum(n8_all, axis=0)
    start = jnp.cumsum(cnt) - cnt
    base = start[None, :] + jnp.cumsum(n8_all, axis=0) - n8_all
    items = (cnt + BM - 1) // BM
    ends = jnp.cumsum(items)
    return dict(n8=n8_all, off=off, base=base, tot=tot, cnt=cnt, start=start, items=items, ends=ends)


def kernel(x_prompt, x_sample, cache_pool, c_prompt, c_sample, w_ada, b_ada, g_pre_mix, g_post_mix, g_pre_ffn, g_post_ffn, w_in, w_pool, s_pool, g_v, b_v, w_s, b_s, p_a, p_b, w_o, w_router, b_router, w_up, b_up, w_down, b_down):
    assert w_ada.shape[0] == 1, "single-layer trunk"
    n_p, n_s = x_prompt.shape[0], x_sample.shape[0]
    past_len = x_prompt.shape[1]
    i32 = jnp.int32

    mod = _ada(jnp.concatenate([c_prompt, c_sample], axis=0), w_ada.reshape(D_MODEL, N_MOD * D_MODEL), b_ada)
    mod = mod.reshape(n_p + n_s, N_MOD, D_MODEL)
    mod_p, mod_s = mod[:n_p], mod[n_p:]

    row = lambda v: v.reshape(1, -1)
    weights = (row(g_pre_mix[0]), w_in[0].astype(BF16), w_pool[0].astype(BF16), row(s_pool[0]), row(g_v[0]),
               row(b_v[0]), w_s[0].astype(BF16), b_s[0].T, p_a[0].astype(BF16), p_b[0].astype(BF16),
               w_o[0].astype(BF16), row(g_post_mix[0]), row(g_pre_ffn[0]), w_router[0].T.astype(BF16),
               b_router[0].reshape(-1, 1))

    sub_s = x_sample.shape[1]
    x1_p, h2_p, pool_p, info_p, n8_p = _mixer(x_prompt, mod_p, None, weights, n_sub=1, sub_len=TM, pos0=0)
    x1_s, h2_s, pool_s, v_s, info_s, n8_s = _mixer(x_sample, mod_s, cache_pool[0], weights,
                                                   n_sub=TM // sub_s, sub_len=sub_s, pos0=past_len)

    tiles_p, tiles_s = n8_p.shape[0], n8_s.shape[0]
    n_tiles = tiles_p + tiles_s
    tb = _tables(jnp.concatenate([n8_p, n8_s], axis=0).reshape(n_tiles, N_EXPERTS))
    n_rows = n_tiles * (TM * TOP_K + N_EXPERTS * (ROW_ALIGN - 1))
    n_items = n_rows // BM + N_EXPERTS

    def tile_tables(lo, hi):
        return (tb["n8"][lo:hi].reshape(-1).astype(i32), tb["off"][lo:hi].reshape(-1).astype(i32),
                tb["base"][lo:hi].reshape(-1).astype(i32), tb["tot"][lo:hi].astype(i32))

    xb = _dispatch(h2_p, info_p, tile_tables(0, tiles_p), None, n_sub=1, sub_len=TM, n_rows=n_rows)
    xb = _dispatch(h2_s, info_s, tile_tables(tiles_p, n_tiles), xb,
                   n_sub=TM // sub_s, sub_len=sub_s, n_rows=n_rows)

    item = jnp.arange(n_items, dtype=i32)[:, None]
    first = (tb["ends"] - tb["items"])[None, :]
    mine = ((first <= item) & (item < tb["ends"][None, :])).astype(i32)
    done = (item - first) * BM
    item_row = jnp.sum(mine * (tb["start"][None, :] + done), axis=1).astype(i32)
    item_rows = jnp.clip(jnp.sum(mine * (tb["cnt"][None, :] - done), axis=1), 0, BM).astype(i32)
    yb = _experts(xb, tb["ends"].astype(i32), item_row, item_rows, w_up.reshape(N_EXPERTS, D_MODEL, 2 * D_FF),
                  b_up.reshape(N_EXPERTS, 1, 2 * D_FF), w_down.reshape(N_EXPERTS, D_FF, D_MODEL),
                  b_down.reshape(N_EXPERTS, 1, D_MODEL))

    g_post = row(g_post_ffn[0])
    y_p = _combine(yb, info_p, x1_p, mod_p, g_post, tile_tables(0, tiles_p), n_sub=1, sub_len=TM)
    y_s = _combine(yb, info_s, x1_s, mod_s, g_post, tile_tables(tiles_p, n_tiles),
                   n_sub=TM // sub_s, sub_len=sub_s)

    return (y_p, y_s, pool_p[None], pool_s[None], v_s.reshape(1, n_s, sub_s, GMLP_WIDTH))
```

```python
import functools

import jax
import jax.numpy as jnp
from jax import lax
from jax.experimental import pallas as pl
from jax.experimental.pallas import tpu as pltpu

D_MODEL = 1024
POOL_WINDOWS = (2, 4, 8, 16)
POOL_MAX = 16
POOL_GDIM = 128
POOL_WIDTH = 512
GMLP_HEADS = 4
GMLP_CHUNK = 128
GMLP_WIDTH = 512
N_EXPERTS = 32
TOP_K = 4
D_FF = 1024
SWIGLU_LIMIT = 7.0
SWIGLU_ALPHA = 1.702
NORM_EPS = 1e-6
LN_EPS = 1e-5
N_MOD = 6

TM = 256
ROW_ALIGN = 8
RLOC = 1280
RTRASH = N_EXPERTS * ROW_ALIGN
BM = 512
X_SLOTS = 3
MIXER_CHAINS = 4
SORT_CHAINS = 2
VMEM_LIMIT = 56 * 1024 * 1024
ROW_WORDS = D_MODEL // 2
ROW_DTYPE = jnp.uint32

BF16 = jnp.bfloat16
F32 = jnp.float32


def _dot(a, b):
    return jnp.dot(a, b, preferred_element_type=F32)


def _rms(x, g):
    ms = jnp.mean(x * x, axis=-1, keepdims=True)
    return x * lax.rsqrt(ms + NORM_EPS) * g


def _sigmoid(x):
    return 0.5 * jnp.tanh(0.5 * x) + 0.5


def _gelu(x):
    return 0.5 * x * (1.0 + lax.erf(x * 0.7071067811865476))


def _ada_body(c_ref, w_ref, b_ref, o_ref):
    c = c_ref[...]
    s = c * jax.nn.sigmoid(c)
    o_ref[...] = _dot(s.astype(BF16), w_ref[...].astype(BF16)) + b_ref[...]


def _ada(c_all, w_ada, b_ada):
    n = c_all.shape[0]
    nt = N_MOD * D_MODEL // D_MODEL
    return pl.pallas_call(
        _ada_body,
        grid=(nt,),
        in_specs=[pl.BlockSpec((n, D_MODEL), lambda i: (0, 0)),
                  pl.BlockSpec((D_MODEL, D_MODEL), lambda i: (0, i)),
                  pl.BlockSpec((1, D_MODEL), lambda i: (0, i))],
        out_specs=pl.BlockSpec((n, D_MODEL), lambda i: (0, i)),
        out_shape=jax.ShapeDtypeStruct((n, N_MOD * D_MODEL), F32),
        compiler_params=pltpu.CompilerParams(dimension_semantics=("arbitrary",)),
        name="ada",
    )(c_all, w_ada, b_ada)


def _interleave(tiles):
    live = [True] * len(tiles)
    t = 0
    while any(live):
        for c, tile in enumerate(tiles):
            if live[c] and t >= c:
                live[c] = next(tile, None) is not None
        t += 1


def _mixer_body(n_chain, n_sub, sub_len, has_hist, pos0, *refs):
    _interleave([_mixer_tile(c, n_chain, n_sub, sub_len, has_hist, pos0, *refs) for c in range(n_chain)])


def _mixer_tile(c, n_chain, n_sub, sub_len, has_hist, pos0, *refs):
    it = iter(refs)
    x_ref = next(it)
    mod_ref = next(it)
    hist_ref = next(it) if has_hist else None
    (gpre_ref, win_ref, wpool_ref, spool_ref, gv_ref, bv_ref, ws_ref, bst_ref, pa_ref, pb_ref, wo_ref,
     gpost_ref, gffn_ref, wr_ref, br_ref) = [next(it) for _ in range(15)]
    x1_ref = next(it)
    h2_ref = next(it)
    pool_ref = next(it)
    v_ref = next(it) if has_hist else None
    info_ref = next(it)
    n8_ref = next(it)
    ext_ref = next(it)

    if has_hist:
        seqs, rows = slice(c * n_sub, (c + 1) * n_sub), slice(None)
        mod = mod_ref[seqs]
    else:
        seqs, rows = slice(None), slice(c * sub_len, (c + 1) * sub_len)
        mod = mod_ref[...]
    x = x_ref[seqs, rows, :]
    shift_m, scale_m, gate_m = mod[:, 0:1], mod[:, 1:2], mod[:, 2:3]
    shift_f, scale_f = mod[:, 3:4], mod[:, 4:5]

    h = _rms(x, gpre_ref[...]) * (1.0 + scale_m) + shift_m
    hb = h.reshape(TM, D_MODEL).astype(BF16)

    a = _dot(hb, win_ref[:, 0:POOL_WIDTH])
    yield True
    e0 = c * n_sub
    if has_hist:
        for s in range(n_sub):
            ext_ref[e0 + s, 0:1, :] = jnp.zeros((1, POOL_WIDTH), F32)
            ext_ref[e0 + s, 1:POOL_MAX, :] = hist_ref[e0 + s]
            ext_ref[e0 + s, POOL_MAX:POOL_MAX + sub_len, :] = a[s * sub_len:(s + 1) * sub_len]
            pool_ref[e0 + s] = a[(s + 1) * sub_len - (POOL_MAX - 1):(s + 1) * sub_len]
        row0 = pos0
    else:
        j = pl.program_id(1)
        prev = (c - 1) % n_chain
        if c == 0:
            @pl.when(j == 0)
            def _():
                ext_ref[0, 0:POOL_MAX, :] = jnp.zeros((POOL_MAX, POOL_WIDTH), F32)

            @pl.when(j > 0)
            def _():
                ext_ref[0, 0:POOL_MAX, :] = ext_ref[prev, sub_len:sub_len + POOL_MAX, :]
        else:
            ext_ref[c, 0:POOL_MAX, :] = ext_ref[prev, sub_len:sub_len + POOL_MAX, :]

        ext_ref[c, POOL_MAX:POOL_MAX + sub_len, :] = a
        if c == n_chain - 1:
            pool_ref[0] = a[sub_len - (POOL_MAX - 1):sub_len]
        row0 = pos0 + (j * n_chain + c) * sub_len

    pos = row0 + lax.broadcasted_iota(jnp.int32, (sub_len, 1), 0)
    spool = spool_ref[...]
    ya_rows = []
    for s in range(n_sub):
        ya_groups = []
        for g, w in enumerate(POOL_WINDOWS):
            lanes = slice(g * POOL_GDIM, (g + 1) * POOL_GDIM)
            win = ext_ref[e0 + s, POOL_MAX:POOL_MAX + sub_len, lanes]
            for back in range(1, w):
                win = win + ext_ref[e0 + s, POOL_MAX - back:POOL_MAX - back + sub_len, lanes]
            cnt = jnp.minimum(pos + 1, w).astype(F32)
            dlt = win / cnt - a[s * sub_len:(s + 1) * sub_len, lanes]
            ya_groups.append(_dot(dlt.astype(BF16), wpool_ref[g]) * spool[:, lanes])
        ya_rows.append(jnp.concatenate(ya_groups, axis=1))
    y_a = jnp.concatenate(ya_rows, axis=0) if n_sub > 1 else ya_rows[0]
    yield True

    u = _dot(hb, win_ref[:, POOL_WIDTH:POOL_WIDTH + GMLP_WIDTH])
    v = _dot(hb, win_ref[:, POOL_WIDTH + GMLP_WIDTH:POOL_WIDTH + 2 * GMLP_WIDTH])
    yield True
    u = _gelu(u)
    v = _gelu(v)
    mu = jnp.mean(v, axis=-1, keepdims=True)
    vc = v - mu
    var = jnp.mean(vc * vc, axis=-1, keepdims=True)
    vn = vc * lax.rsqrt(var + LN_EPS) * gv_ref[...] + bv_ref[...]
    if has_hist:
        v_ref[c * TM:(c + 1) * TM, :] = vn
    vb = vn.astype(BF16)
    seg = min(sub_len, GMLP_CHUNK)
    tri = (lax.broadcasted_iota(jnp.int32, (seg, seg), 0) >= lax.broadcasted_iota(jnp.int32, (seg, seg), 1))
    bst = bst_ref[...]
    yb_rows = []
    for q in range(TM // seg):
        srows = slice(q * seg, (q + 1) * seg)
        yb_heads = []
        for g in range(GMLP_HEADS):
            lanes = slice(g * 128, (g + 1) * 128)
            wm = jnp.where(tri, ws_ref[g, 0:seg, 0:seg], jnp.zeros((), BF16))
            sg = _dot(wm, vb[srows, lanes]) + bst[0:seg, g:g + 1]
            yb_heads.append(u[srows, lanes] * sg)
        yb_rows.append(jnp.concatenate(yb_heads, axis=1))
    y_b = jnp.concatenate(yb_rows, axis=0)
    yield True

    c0 = POOL_WIDTH + 2 * GMLP_WIDTH
    ga = _dot(hb, win_ref[:, c0:c0 + D_MODEL])
    gb = _dot(hb, win_ref[:, c0 + D_MODEL:c0 + 2 * D_MODEL])
    yield True
    ma = _dot(y_a.astype(BF16), pa_ref[...])
    mb = _dot(y_b.astype(BF16), pb_ref[...])
    yield True
    m = _sigmoid(ga) * ma + _sigmoid(gb) * mb
    yield True
    mo = _dot(m.astype(BF16), wo_ref[...]).reshape(n_sub, sub_len, D_MODEL)
    yield True
    x1 = x + gate_m * _rms(mo, gpost_ref[...])
    x1_ref[seqs, rows, :] = x1

    h2 = (_rms(x1, gffn_ref[...]) * (1.0 + scale_f) + shift_f).astype(BF16)
    h2_ref[seqs, rows, :] = h2
    yield True
    logits = lax.dot_general(wr_ref[...], h2.reshape(TM, D_MODEL), (((1,), (1,)), ((), ())),
                             preferred_element_type=F32) + br_ref[...]

    erow = lax.broadcasted_iota(jnp.int32, (N_EXPERTS, TM), 0)
    work = logits
    onehots, vals = [], []
    for _ in range(TOP_K):
        mx = jnp.max(work, axis=0, keepdims=True)
        idx = jnp.min(jnp.where(work == mx, erow, N_EXPERTS), axis=0, keepdims=True)
        oh = erow == idx
        onehots.append(oh)
        vals.append(mx)
        work = jnp.where(oh, -jnp.inf, work)
    exps = [jnp.exp(vk - vals[0]) for vk in vals]
    den = exps[0] + exps[1] + exps[2] + exps[3]
    gates = [ek / den for ek in exps]

    sel = jnp.zeros((N_EXPERTS, TM), F32)
    for oh in onehots:
        sel = sel + oh.astype(F32)
    before = (lax.broadcasted_iota(jnp.int32, (TM, TM), 0) < lax.broadcasted_iota(jnp.int32, (TM, TM), 1))
    rank = _dot(sel.astype(BF16), before.astype(BF16))
    n_e = jnp.sum(sel, axis=1, keepdims=True).astype(jnp.int32)
    n8 = ((n_e + (ROW_ALIGN - 1)) // ROW_ALIGN) * ROW_ALIGN
    lower = (lax.broadcasted_iota(jnp.int32, (N_EXPERTS, N_EXPERTS), 0)
             > lax.broadcasted_iota(jnp.int32, (N_EXPERTS, N_EXPERTS), 1))
    n8b = jnp.broadcast_to(n8.astype(F32), (N_EXPERTS, 128)).astype(BF16)
    off = _dot(lower.astype(BF16), n8b)[:, 0:1]
    slot = off + rank
    irow = lax.broadcasted_iota(jnp.int32, (2 * TOP_K, TM), 0)
    info = jnp.zeros((2 * TOP_K, TM), F32)
    for k in range(TOP_K):
        p_k = jnp.sum(jnp.where(onehots[k], slot, 0.0), axis=0, keepdims=True)
        info = info + jnp.where(irow == k, p_k, 0.0) + jnp.where(irow == TOP_K + k, gates[k], 0.0)
    info_ref[c * 2 * TOP_K:(c + 1) * 2 * TOP_K, :] = info
    n8_ref[c] = n8


def _mixer(x, mod, hist, weights, *, n_sub, sub_len, pos0):
    n_seq, seq_len, _ = x.shape
    has_hist = hist is not None
    n_chain = MIXER_CHAINS
    assert n_sub * sub_len == TM
    if has_hist:
        assert seq_len == sub_len and n_seq % (n_sub * n_chain) == 0
        blk_seq, blk_rows = n_sub * n_chain, sub_len
        grid = (n_seq // blk_seq, 1)
        xmap = lambda i, j: (i, 0, 0)
        step = lambda i, j: i
    else:
        assert n_sub == 1 and seq_len % (sub_len * n_chain) == 0
        blk_seq, blk_rows = 1, sub_len * n_chain
        steps_per_seq = seq_len // blk_rows
        grid = (n_seq, steps_per_seq)
        xmap = lambda i, j: (i, j, 0)
        step = lambda i, j: i * steps_per_seq + j
    n_tiles = grid[0] * grid[1] * n_chain

    def full(arr):
        nd = arr.ndim
        return pl.BlockSpec(arr.shape, lambda i, j, _nd=nd: (0,) * _nd, pipeline_mode=pl.Buffered(1))

    in_specs = [pl.BlockSpec((blk_seq, blk_rows, D_MODEL), xmap),
                pl.BlockSpec((blk_seq, N_MOD, D_MODEL), lambda i, j: (i, 0, 0))]
    args = [x, mod]
    if has_hist:
        in_specs.append(pl.BlockSpec((blk_seq, POOL_MAX - 1, POOL_WIDTH), lambda i, j: (i, 0, 0)))
        args.append(hist)
    in_specs += [full(w) for w in weights]
    args += list(weights)

    out_shape = [jax.ShapeDtypeStruct(x.shape, F32),
                 jax.ShapeDtypeStruct(x.shape, BF16),
                 jax.ShapeDtypeStruct((n_seq, POOL_MAX - 1, POOL_WIDTH), F32)]
    out_specs = [pl.BlockSpec((blk_seq, blk_rows, D_MODEL), xmap),
                 pl.BlockSpec((blk_seq, blk_rows, D_MODEL), xmap),
                 pl.BlockSpec((blk_seq, POOL_MAX - 1, POOL_WIDTH), lambda i, j: (i, 0, 0))]
    if has_hist:
        out_shape.append(jax.ShapeDtypeStruct((n_seq * seq_len, GMLP_WIDTH), F32))
        out_specs.append(pl.BlockSpec((n_chain * TM, GMLP_WIDTH), lambda i, j: (i, 0)))
    out_shape += [jax.ShapeDtypeStruct((n_tiles * 2 * TOP_K, TM), F32),
                  jax.ShapeDtypeStruct((n_tiles, N_EXPERTS, 1), jnp.int32)]
    out_specs += [pl.BlockSpec((n_chain * 2 * TOP_K, TM), lambda i, j: (step(i, j), 0)),
                  pl.BlockSpec((n_chain, N_EXPERTS, 1), lambda i, j: (step(i, j), 0, 0))]

    return pl.pallas_call(
        functools.partial(_mixer_body, n_chain, n_sub, sub_len, has_hist, pos0),
        grid=grid,
        in_specs=in_specs,
        out_specs=out_specs,
        out_shape=out_shape,
        scratch_shapes=[pltpu.VMEM((n_chain * n_sub, POOL_MAX + sub_len, POOL_WIDTH), F32)],
        compiler_params=pltpu.CompilerParams(dimension_semantics=("arbitrary", "arbitrary"),
                                             vmem_limit_bytes=VMEM_LIMIT),
        name="mixer_hist" if has_hist else "mixer",
    )(*args)


def _slot_matrix(info, weights):
    rows = lax.broadcasted_iota(jnp.int32, (RLOC, TM), 0)
    out = jnp.zeros((RLOC, TM), F32)
    for k in range(TOP_K):
        p_k = info[k:k + 1, :].astype(jnp.int32)
        out = out + jnp.where(p_k == rows, weights[k], 0.0)
    return out.astype(BF16)


def _run_copy(src_ref, src_row, dst_ref, dst_row, n_rows, sem):
    return pltpu.make_async_copy(src_ref.at[pl.ds(src_row, n_rows), :],
                                 dst_ref.at[pl.ds(dst_row, n_rows), :], sem)


def _pack_rows(x):
    half = x.shape[1] // 2
    return pltpu.pack_elementwise([x[:, :half], x[:, half:]], packed_dtype=BF16)


def _unpack_rows(w):
    lo = pltpu.unpack_elementwise(w, index=0, packed_dtype=BF16, unpacked_dtype=F32)
    hi = pltpu.unpack_elementwise(w, index=1, packed_dtype=BF16, unpacked_dtype=F32)
    return jnp.concatenate([lo, hi], axis=1).astype(BF16)


def _tile_rows(ref, c, n_sub):
    if n_sub == 1:
        return (slice(None), slice(c * TM, (c + 1) * TM), slice(None))
    return (slice(c * n_sub, (c + 1) * n_sub), slice(None), slice(None))


def _dispatch_body(n_steps, aliased, n_sub, n8_s, off_s, base_s, tot_s, *refs):
    if aliased:
        h2_ref, info_ref, _xb_in, xb_ref, xs_ref, sems = refs
    else:
        h2_ref, info_ref, xb_ref, xs_ref, sems = refs
    i = pl.program_id(0)
    par = i % 2

    def wait_tile(t, sl):
        tot = pl.multiple_of(tot_s[t], ROW_ALIGN)
        _run_copy(xs_ref.at[sl], 0, xb_ref, 0, tot, sems.at[sl]).wait()

    def wait_step(step, parity):
        for c in range(SORT_CHAINS):
            wait_tile(step * SORT_CHAINS + c, parity * SORT_CHAINS + c)

    @pl.when(i >= 2)
    def _():
        wait_step(i - 2, par)

    def tile(c):
        t = i * SORT_CHAINS + c
        sl = par * SORT_CHAINS + c
        info = info_ref[c * 2 * TOP_K:(c + 1) * 2 * TOP_K, :]
        perm = _slot_matrix(info, [1.0] * TOP_K)
        yield True
        h2 = h2_ref[_tile_rows(h2_ref, c, n_sub)].reshape(TM, D_MODEL)
        xs_ref[sl, 0:RLOC, :] = _pack_rows(_dot(perm, h2))

    _interleave([tile(c) for c in range(SORT_CHAINS)])

    for c in range(SORT_CHAINS):
        t = i * SORT_CHAINS + c
        sl = par * SORT_CHAINS + c
        for e in range(N_EXPERTS):
            n = pl.multiple_of(n8_s[t * N_EXPERTS + e], ROW_ALIGN)
            src = pl.multiple_of(off_s[t * N_EXPERTS + e], ROW_ALIGN)
            dst = pl.multiple_of(base_s[t * N_EXPERTS + e], ROW_ALIGN)
            _run_copy(xs_ref.at[sl], src, xb_ref, dst, n, sems.at[sl]).start()

    @pl.when(i == n_steps - 1)
    def _():
        if n_steps >= 2:
            wait_step(i - 1, 1 - par)
        wait_step(i, par)


def _step_maps(n_seq, seq_len, n_sub, sub_len):
    if n_sub == 1:
        rows = SORT_CHAINS * TM
        assert seq_len % rows == 0
        per_seq = seq_len // rows
        return n_seq * per_seq, (1, rows), (lambda i, *_: (i // per_seq, i % per_seq, 0)), \
            (lambda i, *_: (i // per_seq, 0, 0))
    seqs = SORT_CHAINS * n_sub
    assert seq_len == sub_len and n_seq % seqs == 0
    return n_seq // seqs, (seqs, sub_len), (lambda i, *_: (i, 0, 0)), (lambda i, *_: (i, 0, 0))


def _dispatch(h2, info, tables, xb, *, n_sub, sub_len, n_rows):
    n_seq, seq_len, _ = h2.shape
    n_steps, blk, hmap, _ = _step_maps(n_seq, seq_len, n_sub, sub_len)
    in_specs = [pl.BlockSpec(blk + (D_MODEL,), hmap),
                pl.BlockSpec((SORT_CHAINS * 2 * TOP_K, TM), lambda i, *_: (i, 0))]
    args = [h2, info]
    aliases = {}
    if xb is not None:
        in_specs.append(pl.BlockSpec(memory_space=pl.ANY))
        args.append(xb)
        aliases = {len(tables) + 2: 0}
    return pl.pallas_call(
        functools.partial(_dispatch_body, n_steps, xb is not None, n_sub),
        grid_spec=pltpu.PrefetchScalarGridSpec(
            num_scalar_prefetch=len(tables),
            grid=(n_steps,),
            in_specs=in_specs,
            out_specs=pl.BlockSpec(memory_space=pl.ANY),
            scratch_shapes=[pltpu.VMEM((2 * SORT_CHAINS, RLOC + RTRASH, ROW_WORDS), ROW_DTYPE),
                            pltpu.SemaphoreType.DMA((2 * SORT_CHAINS,))],
        ),
        out_shape=jax.ShapeDtypeStruct((n_rows, ROW_WORDS), ROW_DTYPE),
        input_output_aliases=aliases,
        compiler_params=pltpu.CompilerParams(dimension_semantics=("arbitrary",),
                                             vmem_limit_bytes=VMEM_LIMIT),
        name="dispatch_more" if xb is not None else "dispatch",
    )(*tables, *args)


def _expert_rows(xv, wup_bf, bup, wdn_bf, bdn):
    z = _dot(_unpack_rows(xv), wup_bf[...]) + bup
    zg = jnp.minimum(z[:, :D_FF], SWIGLU_LIMIT)
    zl = jnp.clip(z[:, D_FF:], -SWIGLU_LIMIT, SWIGLU_LIMIT)
    act = zg * _sigmoid(SWIGLU_ALPHA * zg) * (zl + 1.0)
    return _pack_rows(_dot(act.astype(BF16), wdn_bf[...]) + bdn)


def _expert_body(ends_s, row_s, rows_s, xb_ref, wup_ref, bup_ref, wdn_ref, bdn_ref, yb_ref, x_buf, y_buf,
                 wup_bf, wdn_bf, x_sem, y_sem):
    e = pl.program_id(0)
    n_total = ends_s[N_EXPERTS - 1]
    lo = jnp.where(e == 0, 0, ends_s[jnp.maximum(e - 1, 0)])
    hi = ends_s[e]

    def x_copy(item):
        n = pl.multiple_of(rows_s[item], ROW_ALIGN)
        row = pl.multiple_of(row_s[item], ROW_ALIGN)
        slot = item % X_SLOTS
        return _run_copy(xb_ref, row, x_buf.at[slot], 0, n, x_sem.at[slot])

    def y_copy(item):
        n = pl.multiple_of(rows_s[item], ROW_ALIGN)
        row = pl.multiple_of(row_s[item], ROW_ALIGN)
        slot = item % 2
        return _run_copy(y_buf.at[slot], 0, yb_ref, row, n, y_sem.at[slot])

    @pl.when(e == 0)
    def _():
        x_buf[...] = jnp.zeros(x_buf.shape, ROW_DTYPE)
        for ahead in range(X_SLOTS - 1):
            @pl.when(ahead < n_total)
            def _(ahead=ahead):
                x_copy(ahead).start()

    @pl.when(hi > lo)
    def _():
        wup_bf[...] = wup_ref[...].astype(BF16)
        wdn_bf[...] = wdn_ref[...].astype(BF16)
        bup = bup_ref[...]
        bdn = bdn_ref[...]

        def run_item(item, carry):
            xs = item % X_SLOTS
            ys = item % 2
            x_copy(item).wait()

            @pl.when(item + (X_SLOTS - 1) < n_total)
            def _():
                x_copy(item + (X_SLOTS - 1)).start()

            @pl.when(item >= 2)
            def _():
                y_copy(item - 2).wait()

            half = BM // 2

            @pl.when(rows_s[item] > half)
            def _():
                y_buf[ys] = _expert_rows(x_buf[xs], wup_bf, bup, wdn_bf, bdn)

            @pl.when(rows_s[item] <= half)
            def _():
                y_buf[ys, 0:half, :] = _expert_rows(x_buf[xs, 0:half, :], wup_bf, bup, wdn_bf, bdn)

            y_copy(item).start()
            return carry

        lax.fori_loop(lo, hi, run_item, 0)

    @pl.when(e == N_EXPERTS - 1)
    def _():
        @pl.when(n_total >= 2)
        def _():
            y_copy(n_total - 2).wait()

        y_copy(n_total - 1).wait()


def _experts(xb, ends, item_row, item_rows, w_up, b_up, w_down, b_down):
    n_rows = xb.shape[0]
    wmap = lambda e, *_: (e, 0, 0)
    return pl.pallas_call(
        _expert_body,
        grid_spec=pltpu.PrefetchScalarGridSpec(
            num_scalar_prefetch=3,
            grid=(N_EXPERTS,),
            in_specs=[pl.BlockSpec(memory_space=pl.ANY),
                      pl.BlockSpec((None, D_MODEL, 2 * D_FF), wmap),
                      pl.BlockSpec((None, 1, 2 * D_FF), wmap),
                      pl.BlockSpec((None, D_FF, D_MODEL), wmap),
                      pl.BlockSpec((None, 1, D_MODEL), wmap)],
            out_specs=pl.BlockSpec(memory_space=pl.ANY),
            scratch_shapes=[pltpu.VMEM((X_SLOTS, BM, ROW_WORDS), ROW_DTYPE),
                            pltpu.VMEM((2, BM, ROW_WORDS), ROW_DTYPE),
                            pltpu.VMEM((D_MODEL, 2 * D_FF), BF16),
                            pltpu.VMEM((D_FF, D_MODEL), BF16),
                            pltpu.SemaphoreType.DMA((X_SLOTS,)),
                            pltpu.SemaphoreType.DMA((2,))],
        ),
        out_shape=jax.ShapeDtypeStruct((n_rows, ROW_WORDS), ROW_DTYPE),
        compiler_params=pltpu.CompilerParams(dimension_semantics=("arbitrary",),
                                             vmem_limit_bytes=VMEM_LIMIT),
        name="experts",
    )(ends, item_row, item_rows, xb, w_up, b_up, w_down, b_down)


def _combine_body(n_steps, n_sub, n8_s, off_s, base_s, tot_s, yb_ref, info_ref, x1_ref, mod_ref, gpost_ref,
                  o_ref, yloc_ref, sems):
    i = pl.program_id(0)
    par = i % 2

    def start_step(step, parity):
        for c in range(SORT_CHAINS):
            t = step * SORT_CHAINS + c
            sl = parity * SORT_CHAINS + c
            for e in range(N_EXPERTS):
                n = pl.multiple_of(n8_s[t * N_EXPERTS + e], ROW_ALIGN)
                src = pl.multiple_of(base_s[t * N_EXPERTS + e], ROW_ALIGN)
                dst = pl.multiple_of(off_s[t * N_EXPERTS + e], ROW_ALIGN)
                _run_copy(yb_ref, src, yloc_ref.at[sl], dst, n, sems.at[sl]).start()

    @pl.when(i == 0)
    def _():
        yloc_ref[:, TM * TOP_K:RLOC, :] = jnp.zeros((2 * SORT_CHAINS, RLOC - TM * TOP_K, ROW_WORDS), ROW_DTYPE)
        start_step(0, 0)

    @pl.when(i + 1 < n_steps)
    def _():
        start_step(i + 1, 1 - par)

    for c in range(SORT_CHAINS):
        tot = pl.multiple_of(tot_s[i * SORT_CHAINS + c], ROW_ALIGN)
        _run_copy(yb_ref, 0, yloc_ref.at[par * SORT_CHAINS + c], 0, tot, sems.at[par * SORT_CHAINS + c]).wait()

    def tile(c):
        sl = par * SORT_CHAINS + c
        info = info_ref[c * 2 * TOP_K:(c + 1) * 2 * TOP_K, :]
        wmat = _slot_matrix(info, [info[TOP_K + k:TOP_K + k + 1, :] for k in range(TOP_K)])
        yield True
        f = lax.dot_general(wmat, _unpack_rows(yloc_ref[sl, 0:RLOC, :]), (((0,), (0,)), ((), ())),
                            preferred_element_type=F32)
        yield True
        where = _tile_rows(x1_ref, c, n_sub)
        x1 = x1_ref[where]
        gate_f = (mod_ref[...] if n_sub == 1 else mod_ref[where[0]])[:, 5:6]
        o_ref[where] = x1 + gate_f * _rms(f.reshape(x1.shape), gpost_ref[...])

    _interleave([tile(c) for c in range(SORT_CHAINS)])


def _combine(yb, info, x1, mod, g_post, tables, *, n_sub, sub_len):
    n_seq, seq_len, _ = x1.shape
    n_steps, blk, xmap, mmap = _step_maps(n_seq, seq_len, n_sub, sub_len)
    return pl.pallas_call(
        functools.partial(_combine_body, n_steps, n_sub),
        grid_spec=pltpu.PrefetchScalarGridSpec(
            num_scalar_prefetch=len(tables),
            grid=(n_steps,),
            in_specs=[pl.BlockSpec(memory_space=pl.ANY),
                      pl.BlockSpec((SORT_CHAINS * 2 * TOP_K, TM), lambda i, *_: (i, 0)),
                      pl.BlockSpec(blk + (D_MODEL,), xmap),
                      pl.BlockSpec((blk[0], N_MOD, D_MODEL), mmap),
                      pl.BlockSpec((1, D_MODEL), lambda i, *_: (0, 0))],
            out_specs=pl.BlockSpec(blk + (D_MODEL,), xmap),
            scratch_shapes=[pltpu.VMEM((2 * SORT_CHAINS, RLOC + RTRASH, ROW_WORDS), ROW_DTYPE),
                            pltpu.SemaphoreType.DMA((2 * SORT_CHAINS,))],
        ),
        out_shape=jax.ShapeDtypeStruct(x1.shape, F32),
        compiler_params=pltpu.CompilerParams(dimension_semantics=("arbitrary",),
                                             vmem_limit_bytes=VMEM_LIMIT),
        name="combine",
    )(*tables, yb, info, x1, mod, g_post)


def _tables(n8_all):
    n_tiles = n8_all.shape[0]
    off = jnp.cumsum(n8_all, axis=1) - n8_all
    tot = jnp.sum(n8_all, axis=1)
    cnt = jnp.sum(n8_all, axis=0)
    start = jnp.cumsum(cnt) - cnt
    base = start[None, :] + jnp.cumsum(n8_all, axis=0) - n8_all
    items = (cnt + BM - 1) // BM
    ends = jnp.cumsum(items)
    return dict(n8=n8_all, off=off, base=base, tot=tot, cnt=cnt, start=start, items=items, ends=ends)


def kernel(x_prompt, x_sample, cache_pool, c_prompt, c_sample, w_ada, b_ada, g_pre_mix, g_post_mix, g_pre_ffn, g_post_ffn, w_in, w_pool, s_pool, g_v, b_v, w_s, b_s, p_a, p_b, w_o, w_router, b_router, w_up, b_up, w_down, b_down):
    assert w_ada.shape[0] == 1, "single-layer trunk"
    n_p, n_s = x_prompt.shape[0], x_sample.shape[0]
    past_len = x_prompt.shape[1]
    i32 = jnp.int32

    mod = _ada(jnp.concatenate([c_prompt, c_sample], axis=0), w_ada.reshape(D_MODEL, N_MOD * D_MODEL), b_ada)
    mod = mod.reshape(n_p + n_s, N_MOD, D_MODEL)
    mod_p, mod_s = mod[:n_p], mod[n_p:]

    row = lambda v: v.reshape(1, -1)
    weights = (row(g_pre_mix[0]), w_in[0].astype(BF16), w_pool[0].astype(BF16), row(s_pool[0]), row(g_v[0]),
               row(b_v[0]), w_s[0].astype(BF16), b_s[0].T, p_a[0].astype(BF16), p_b[0].astype(BF16),
               w_o[0].astype(BF16), row(g_post_mix[0]), row(g_pre_ffn[0]), w_router[0].T.astype(BF16),
               b_router[0].reshape(-1, 1))

    sub_s = x_sample.shape[1]
    x1_p, h2_p, pool_p, info_p, n8_p = _mixer(x_prompt, mod_p, None, weights, n_sub=1, sub_len=TM, pos0=0)
    x1_s, h2_s, pool_s, v_s, info_s, n8_s = _mixer(x_sample, mod_s, cache_pool[0], weights,
                                                   n_sub=TM // sub_s, sub_len=sub_s, pos0=past_len)

    tiles_p, tiles_s = n8_p.shape[0], n8_s.shape[0]
    n_tiles = tiles_p + tiles_s
    tb = _tables(jnp.concatenate([n8_p, n8_s], axis=0).reshape(n_tiles, N_EXPERTS))
    n_rows = n_tiles * (TM * TOP_K + N_EXPERTS * (ROW_ALIGN - 1))
    n_items = n_rows // BM + N_EXPERTS

    empty = tb["n8"] == 0
    trash = jnp.arange(N_EXPERTS, dtype=i32)[None, :] * ROW_ALIGN
    n_eff = jnp.where(empty, ROW_ALIGN, tb["n8"]).astype(i32)
    tot_eff = jnp.sum(n_eff, axis=1)

    def tile_tables(lo, hi, local, remote):
        flat = lambda a: a[lo:hi].reshape(-1).astype(i32)
        return (flat(n_eff), flat(local), flat(remote), tot_eff[lo:hi])

    send = (jnp.where(empty, 0, tb["off"]), jnp.where(empty, n_rows + trash, tb["base"]))
    back = (jnp.where(empty, RLOC + trash, tb["off"]), jnp.where(empty, 0, tb["base"]))
    xb = _dispatch(h2_p, info_p, tile_tables(0, tiles_p, *send), None, n_sub=1, sub_len=TM,
                   n_rows=n_rows + RTRASH)
    xb = _dispatch(h2_s, info_s, tile_tables(tiles_p, n_tiles, *send), xb,
                   n_sub=TM // sub_s, sub_len=sub_s, n_rows=n_rows + RTRASH)

    item = jnp.arange(n_items, dtype=i32)[:, None]
    first = (tb["ends"] - tb["items"])[None, :]
    mine = ((first <= item) & (item < tb["ends"][None, :])).astype(i32)
    done = (item - first) * BM
    item_row = jnp.sum(mine * (tb["start"][None, :] + done), axis=1).astype(i32)
    item_rows = jnp.clip(jnp.sum(mine * (tb["cnt"][None, :] - done), axis=1), 0, BM).astype(i32)
    yb = _experts(xb, tb["ends"].astype(i32), item_row, item_rows, w_up.reshape(N_EXPERTS, D_MODEL, 2 * D_FF),
                  b_up.reshape(N_EXPERTS, 1, 2 * D_FF), w_down.reshape(N_EXPERTS, D_FF, D_MODEL),
                  b_down.reshape(N_EXPERTS, 1, D_MODEL))

    g_post = row(g_post_ffn[0])
    y_p = _combine(yb, info_p, x1_p, mod_p, g_post, tile_tables(0, tiles_p, *back), n_sub=1, sub_len=TM)
    y_s = _combine(yb, info_s, x1_s, mod_s, g_post, tile_tables(tiles_p, n_tiles, *back),
                   n_sub=TM // sub_s, sub_len=sub_s)

    return (y_p, y_s, pool_p[None], pool_s[None], v_s.reshape(1, n_s, sub_s, GMLP_WIDTH))
```
